```python
import jax, jax.numpy as jnp
from jax import lax
import numpy as np


D_MODEL = 1024
BATCH = 16
SEQ = 4096
DEPTH = 4
DEC_BATCH = 16
DEC_SEQ = 2048
PAST_LEN = 128

N_MIXERS = 3
GRID_W = 64
EPS = 1e-6
FNET_GROUPS = 4
FNET_GROUP_DIM = D_MODEL // FNET_GROUPS
HEAD_DIM = 128
N_HEADS = D_MODEL // HEAD_DIM
N_KV_HEADS = 2
Q_PER_KV = N_HEADS // N_KV_HEADS
QKV_DIM = (N_HEADS + 2 * N_KV_HEADS) * HEAD_DIM
AXIS_ROT_DIM = HEAD_DIM // 2
ROPE_THETA = 10000.0
Q_BLOCK = 128
SGU_DIM = D_MODEL
SGU_GROUPS = 8
SGU_GROUP_DIM = SGU_DIM // SGU_GROUPS
SGU_CHUNK = 128
FFN_HIDDEN = -(-8 * D_MODEL // (3 * 256)) * 256

kernel_name = 'hybrid_fnet_gqa_gmlp_adaln_encoder'


def rms_norm(x, g):
    xf = x.astype(jnp.float32)
    y = xf * lax.rsqrt(jnp.mean(xf * xf, axis=-1, keepdims=True) + EPS)
    return (y * g.astype(jnp.float32)).astype(x.dtype)


def layer_norm(x, g, b):
    xf = x.astype(jnp.float32)
    mu = jnp.mean(xf, axis=-1, keepdims=True)
    xc = xf - mu
    y = xc * lax.rsqrt(jnp.mean(xc * xc, axis=-1, keepdims=True) + EPS)
    return (y * g.astype(jnp.float32) + b.astype(jnp.float32)).astype(x.dtype)


def modulate(h, shift, scale):
    return h * (1.0 + scale[:, None, :]) + shift[:, None, :]


def fourier_mixer(h, w_o):
    B, S, _ = h.shape
    hg = h.astype(jnp.float32).reshape(B, S, FNET_GROUPS, FNET_GROUP_DIM)
    f = jnp.fft.fftn(hg, axes=(1, 3), norm='ortho').real
    return f.reshape(B, S, D_MODEL).astype(h.dtype) @ w_o


def axial_rope_tables(S):
    rows = S // GRID_W
    row = jnp.repeat(jnp.arange(rows), GRID_W).astype(jnp.float32)
    col = jnp.tile(jnp.arange(GRID_W), rows).astype(jnp.float32)
    freqs = 1.0 / (ROPE_THETA ** (jnp.arange(0, AXIS_ROT_DIM, 2, dtype=jnp.float32) / AXIS_ROT_DIM))
    ang_r = row[:, None] * freqs[None, :]
    ang_c = col[:, None] * freqs[None, :]
    return jnp.cos(ang_r), jnp.sin(ang_r), jnp.cos(ang_c), jnp.sin(ang_c)


def rotate_half_rope(x, cos, sin):
    x1, x2 = jnp.split(x, 2, axis=-1)
    c = cos[:, None, :].astype(x.dtype)
    s = sin[:, None, :].astype(x.dtype)
    return jnp.concatenate([x1 * c - x2 * s, x2 * c + x1 * s], axis=-1)


def apply_axial_rope(x, tabs):
    cos_r, sin_r, cos_c, sin_c = tabs
    xr, xc = jnp.split(x, 2, axis=-1)
    return jnp.concatenate([rotate_half_rope(xr, cos_r, sin_r), rotate_half_rope(xc, cos_c, sin_c)], axis=-1)


def attention_mixer(h, w_qkv, q_g, k_g, w_o):
    B, S, _ = h.shape
    qkv = h @ w_qkv
    q, k, v = jnp.split(qkv, [N_HEADS * HEAD_DIM, (N_HEADS + N_KV_HEADS) * HEAD_DIM], axis=-1)
    q = rms_norm(q.reshape(B, S, N_HEADS, HEAD_DIM), q_g)
    k = rms_norm(k.reshape(B, S, N_KV_HEADS, HEAD_DIM), k_g)
    v = v.reshape(B, S, N_KV_HEADS, HEAD_DIM)
    tabs = axial_rope_tables(S)
    q = apply_axial_rope(q, tabs) * (HEAD_DIM ** -0.5)
    k = apply_axial_rope(k, tabs)
    nb = S // Q_BLOCK
    qb = q.reshape(B, nb, Q_BLOCK, N_KV_HEADS, Q_PER_KV, HEAD_DIM).transpose(1, 0, 2, 3, 4, 5)

    def block(qi):
        s = jnp.einsum('bqkgd,bskd->bkgqs', qi, k).astype(jnp.float32)
        p = jax.nn.softmax(s, axis=-1).astype(v.dtype)
        return jnp.einsum('bkgqs,bskd->bqkgd', p, v)

    o = lax.map(block, qb)
    o = o.transpose(1, 0, 2, 3, 4, 5).reshape(B, S, N_HEADS * HEAD_DIM)
    return o @ w_o


def spatial_gating_mixer(h, w_in, ln_g, ln_b, w_s, b_s, w_o):
    B, S, _ = h.shape
    uv = jax.nn.gelu(h @ w_in, approximate=False)
    u, v = jnp.split(uv, 2, axis=-1)
    v = layer_norm(v, ln_g, ln_b)
    nc = S // SGU_CHUNK
    vc = v.reshape(B, nc, SGU_CHUNK, SGU_GROUPS, SGU_GROUP_DIM)
    sv = jnp.einsum('gij,bnjgc->bnigc', w_s, vc) + b_s.T[None, None, :, :, None]
    return (u * sv.reshape(B, S, SGU_DIM)) @ w_o


def swiglu_ffn(h, w_gu, w_down):
    g, u = jnp.split(h @ w_gu, 2, axis=-1)
    return (jax.nn.silu(g) * u) @ w_down


def run_trunk(x, c, norm1_g, norm2_g, w_ada, b_ada, fnet_w_o, attn_w_qkv, attn_q_g, attn_k_g, attn_w_o,
              sgu_w_in, sgu_ln_g, sgu_ln_b, sgu_w_s, sgu_b_s, sgu_w_o, ffn_w_gu, ffn_w_down, final_g):
    cs = jax.nn.silu(c)
    for i in range(DEPTH):
        mod = cs @ w_ada[i] + b_ada[i]
        sh1, sc1, g1, sh2, sc2, g2 = jnp.split(mod, 6, axis=-1)
        h = modulate(rms_norm(x, norm1_g[i]), sh1, sc1)
        kind = i % N_MIXERS
        j = i // N_MIXERS
        if kind == 0:
            m = fourier_mixer(h, fnet_w_o[j])
        elif kind == 1:
            m = attention_mixer(h, attn_w_qkv[j], attn_q_g[j], attn_k_g[j], attn_w_o[j])
        else:
            m = spatial_gating_mixer(h, sgu_w_in[j], sgu_ln_g[j], sgu_ln_b[j], sgu_w_s[j], sgu_b_s[j], sgu_w_o[j])
        x = x + g1[:, None, :] * m
        h = modulate(rms_norm(x, norm2_g[i]), sh2, sc2)
        x = x + g2[:, None, :] * swiglu_ffn(h, ffn_w_gu[i], ffn_w_down[i])
    return rms_norm(x, final_g)


def setup_inputs(seed: int = 0) -> dict:
    key = jax.random.key(seed)
    ks = jax.random.split(key, 32)
    n_a = (DEPTH + 2) // 3
    n_b = (DEPTH + 1) // 3
    n_c = DEPTH // 3
    f32 = jnp.float32

    def nrm(k, shape, scale):
        return jax.random.normal(k, shape, f32) * scale

    D = D_MODEL
    return {
        'x_prompt': nrm(ks[0], (BATCH, SEQ, D), 1.0),
        'x_sample': nrm(ks[1], (DEC_BATCH, DEC_SEQ, D), 1.0),
        'c_prompt': nrm(ks[2], (BATCH, D), 1.0),
        'c_sample': nrm(ks[3], (DEC_BATCH, D), 1.0),
        'norm1_g': 1.0 + nrm(ks[4], (DEPTH, D), 0.02),
        'norm2_g': 1.0 + nrm(ks[5], (DEPTH, D), 0.02),
        'w_ada': nrm(ks[6], (DEPTH, D, 6 * D), 0.5 * D ** -0.5),
        'b_ada': nrm(ks[7], (DEPTH, 6 * D), 0.01),
        'fnet_w_o': nrm(ks[8], (n_a, D, D), D ** -0.5),
        'attn_w_qkv': nrm(ks[9], (n_b, D, QKV_DIM), D ** -0.5),
        'attn_q_g': 1.0 + nrm(ks[10], (n_b, HEAD_DIM), 0.02),
        'attn_k_g': 1.0 + nrm(ks[11], (n_b, HEAD_DIM), 0.02),
        'attn_w_o': nrm(ks[12], (n_b, N_HEADS * HEAD_DIM, D), (N_HEADS * HEAD_DIM) ** -0.5),
        'sgu_w_in': nrm(ks[13], (n_c, D, 2 * SGU_DIM), D ** -0.5),
        'sgu_ln_g': 1.0 + nrm(ks[14], (n_c, SGU_DIM), 0.02),
        'sgu_ln_b': nrm(ks[15], (n_c, SGU_DIM), 0.02),
        'sgu_w_s': nrm(ks[16], (n_c, SGU_GROUPS, SGU_CHUNK, SGU_CHUNK), 0.5 * SGU_CHUNK ** -0.5),
        'sgu_b_s': 1.0 + nrm(ks[17], (n_c, SGU_GROUPS, SGU_CHUNK), 0.02),
        'sgu_w_o': nrm(ks[18], (n_c, SGU_DIM, D), SGU_DIM ** -0.5),
        'ffn_w_gu': nrm(ks[19], (DEPTH, D, 2 * FFN_HIDDEN), D ** -0.5),
        'ffn_w_down': nrm(ks[20], (DEPTH, FFN_HIDDEN, D), FFN_HIDDEN ** -0.5),
        'final_g': 1.0 + nrm(ks[21], (D,), 0.02),
    }


def reference(x_prompt, x_sample, c_prompt, c_sample, norm1_g, norm2_g, w_ada, b_ada, fnet_w_o,
              attn_w_qkv, attn_q_g, attn_k_g, attn_w_o, sgu_w_in, sgu_ln_g, sgu_ln_b, sgu_w_s, sgu_b_s,
              sgu_w_o, ffn_w_gu, ffn_w_down, final_g):
    y_prompt = run_trunk(x_prompt, c_prompt, norm1_g, norm2_g, w_ada, b_ada, fnet_w_o, attn_w_qkv, attn_q_g,
                         attn_k_g, attn_w_o, sgu_w_in, sgu_ln_g, sgu_ln_b, sgu_w_s, sgu_b_s, sgu_w_o,
                         ffn_w_gu, ffn_w_down, final_g)
    y_sample = run_trunk(x_sample, c_sample, norm1_g, norm2_g, w_ada, b_ada, fnet_w_o, attn_w_qkv, attn_q_g,
                         attn_k_g, attn_w_o, sgu_w_in, sgu_ln_g, sgu_ln_b, sgu_w_s, sgu_b_s, sgu_w_o,
                         ffn_w_gu, ffn_w_down, final_g)
    return (y_prompt, y_sample)
```

```python
import functools
import math

import jax
import jax.numpy as jnp
from jax import lax
from jax.experimental import pallas as pl
from jax.experimental.pallas import tpu as pltpu

F32 = jnp.float32
BF16 = jnp.bfloat16

D_MODEL = 1024
DEPTH = 4
EPS = 1e-6
GRID_W = 64
FNET_GROUPS = 4
FNET_GROUP_DIM = D_MODEL // FNET_GROUPS
HEAD_DIM = 128
N_HEADS = 8
N_KV_HEADS = 2
Q_PER_KV = N_HEADS // N_KV_HEADS
QKV_DIM = (N_HEADS + 2 * N_KV_HEADS) * HEAD_DIM
ROPE_THETA = 10000.0
SGU_GROUPS = 8
SGU_CHUNK = 128
FFN_HIDDEN = 2816

DFT_LONG = 256
LANES = 128
TOKEN_TILE = 512
ATTN_Q_TILE = 256
FFN_CHUNK = 256
VMEM_LIMIT = 56 * 1024 * 1024


def _params(n_axes):
    return pltpu.CompilerParams(
        dimension_semantics=("parallel",) * n_axes, vmem_limit_bytes=VMEM_LIMIT)


def _resident(block_shape, index_map):
    return pl.BlockSpec(block_shape, index_map, pipeline_mode=pl.Buffered(1))


def _rms(x):
    return x * lax.rsqrt(jnp.mean(x * x, axis=-1, keepdims=True) + EPS)


def _norm_mod(x, gain, shift, scale):
    return (_rms(x) * gain) * (1.0 + scale) + shift


def _mod_kernel(c_ref, w_ref, b_ref, o_ref):
    c = c_ref[...]
    cs = c * jax.nn.sigmoid(c)
    o_ref[0] = jnp.dot(cs, w_ref[0], precision=lax.Precision.HIGHEST,
                       preferred_element_type=F32) + b_ref[0]


def _modulation(c_all, w_ada, b_ada):
    n_b = c_all.shape[0]
    tn = 1536
    out = pl.pallas_call(
        _mod_kernel,
        grid=(DEPTH, 6 * D_MODEL // tn),
        in_specs=[
            pl.BlockSpec((n_b, D_MODEL), lambda l, j: (0, 0)),
            pl.BlockSpec((1, D_MODEL, tn), lambda l, j: (l, 0, j)),
            pl.BlockSpec((1, 1, tn), lambda l, j: (l, 0, j)),
        ],
        out_specs=pl.BlockSpec((1, n_b, tn), lambda l, j: (l, 0, j)),
        out_shape=jax.ShapeDtypeStruct((DEPTH, n_b, 6 * D_MODEL), F32),
        compiler_params=_params(2),
        name="adaln_mod",
    )(c_all, w_ada, b_ada.reshape(DEPTH, 1, 6 * D_MODEL))
    return out.reshape(DEPTH, n_b, 6, D_MODEL)


def _mod_spec(layer, b_off):
    return pl.BlockSpec((1, 1, 6, D_MODEL), lambda b, i: (layer, b_off + b, 0, 0))


def _ffn_kernel(x_ref, mod_ref, g_ref, wgu_ref, wd_ref, fg_ref, o_ref, h_scr, acc_scr,
                *, n_chunks, chunk, final):
    x = x_ref[0]
    mod = mod_ref[0, 0]
    h_scr[...] = _norm_mod(x, g_ref[0], mod[3:4], mod[4:5]).astype(BF16)
    acc_scr[...] = jnp.zeros_like(acc_scr)

    def body(j, carry):
        gu = jnp.dot(h_scr[...], wgu_ref[0, j], preferred_element_type=F32)
        g = gu[:, :chunk]
        u = gu[:, chunk:]
        a = (g * jax.nn.sigmoid(g) * u).astype(BF16)
        acc_scr[...] += jnp.dot(a, wd_ref[0, j], preferred_element_type=F32)
        return carry

    lax.fori_loop(0, n_chunks, body, 0)
    y = x + mod[5:6] * acc_scr[...]
    if final:
        y = _rms(y) * fg_ref[...]
    o_ref[0] = y


def _ffn(x, mod, layer, b_off, norm_g, wgu, wd, final_g, final):
    n_b, seq, _ = x.shape
    tm = TOKEN_TILE
    n_chunks = wgu.shape[1]
    chunk = wd.shape[2]
    kern = functools.partial(_ffn_kernel, n_chunks=n_chunks, chunk=chunk, final=final)
    return pl.pallas_call(
        kern,
        grid=(n_b, seq // tm),
        in_specs=[
            pl.BlockSpec((1, tm, D_MODEL), lambda b, i: (b, i, 0)),
            _mod_spec(layer, b_off),
            pl.BlockSpec((1, 1, D_MODEL), lambda b, i: (layer, 0, 0)),
            _resident((1, n_chunks, D_MODEL, 2 * chunk), lambda b, i: (layer, 0, 0, 0)),
            _resident((1, n_chunks, chunk, D_MODEL), lambda b, i: (layer, 0, 0, 0)),
            pl.BlockSpec((1, D_MODEL), lambda b, i: (0, 0)),
        ],
        out_specs=pl.BlockSpec((1, tm, D_MODEL), lambda b, i: (b, i, 0)),
        out_shape=jax.ShapeDtypeStruct(x.shape, F32),
        scratch_shapes=[pltpu.VMEM((tm, D_MODEL), BF16), pltpu.VMEM((tm, D_MODEL), F32)],
        compiler_params=_params(2),
        name="ffn_final" if final else "ffn",
    )(x, mod, norm_g, wgu, wd, final_g)


def _fnet_chan_kernel(x_ref, mod_ref, g_ref, wc_ref, o_ref, ab_scr, *, n2):
    x = x_ref[0]
    mod = mod_ref[0, 0]
    h = _norm_mod(x, g_ref[0], mod[0:1], mod[1:2]).astype(BF16)
    gd = FNET_GROUP_DIM
    per_group = gd // LANES
    for g in range(FNET_GROUPS):
        ab = jnp.dot(h[:, g * gd:(g + 1) * gd], wc_ref[...], preferred_element_type=F32)
        for part in range(2):
            for t in range(per_group):
                ab_scr[part, g * per_group + t] = ab[:, part * gd + t * LANES:part * gd + (t + 1) * LANES]
    rows = x.shape[0] // n2
    for j in range(n2):
        for part in range(2):
            for blk in range(D_MODEL // LANES):
                sel = ab_scr[part, blk, pl.ds(j, rows, stride=n2), :]
                o_ref[0, part, j, :, blk * LANES:(blk + 1) * LANES] = sel.astype(BF16)


def _fnet_stage_a_kernel(ab_ref, wa_ref, o_ref, *, n2):
    ab = ab_ref[0, :, 0].reshape(2 * DFT_LONG, D_MODEL)
    v = jnp.dot(wa_ref[0], ab, preferred_element_type=F32)
    o_ref[0, :, :, 0] = v.reshape(2, n2, DFT_LONG // n2, D_MODEL).astype(BF16)


def _fnet_stage_b_kernel(v_ref, x_ref, mod_ref, mb_ref, wo_ref, o_ref, *, n2, scale):
    k1l = DFT_LONG // n2
    vb = v_ref[0, :, 0].reshape(2 * DFT_LONG, D_MODEL)
    f = jnp.dot(mb_ref[...], vb, preferred_element_type=F32) * scale
    m = jnp.dot(f.astype(BF16), wo_ref[0], preferred_element_type=F32)
    x = x_ref[0, :, 0].reshape(DFT_LONG, D_MODEL)
    mod = mod_ref[0, 0]
    o_ref[0, :, 0] = (x + mod[2:3] * m).reshape(n2, k1l, D_MODEL)


def _fnet_tables(seq):
    n2 = seq // DFT_LONG
    k1l = DFT_LONG // n2
    two_pi = 2.0 * math.pi
    gd = FNET_GROUP_DIM
    c = jnp.arange(gd, dtype=jnp.int32)
    ang = ((c[:, None] * c[None, :]) % gd).astype(F32) * (two_pi / gd)
    inv = 1.0 / math.sqrt(gd)
    wc = jnp.concatenate([jnp.cos(ang) * inv, jnp.sin(ang) * inv], axis=1).astype(BF16)

    k1 = jnp.arange(DFT_LONG, dtype=jnp.int32)
    n1 = jnp.arange(DFT_LONG, dtype=jnp.int32)
    j2 = jnp.arange(n2, dtype=jnp.int32)
    sp = n2 * n1[None, None, :] + j2[:, None, None]
    ang = ((k1[None, :, None] * sp) % seq).astype(F32) * (two_pi / seq)
    ca, sa = jnp.cos(ang), jnp.sin(ang)
    wa = jnp.concatenate([jnp.concatenate([ca, -sa], axis=2),
                          jnp.concatenate([sa, ca], axis=2)], axis=1).astype(BF16)

    ang = ((j2[:, None] * j2[None, :]) % n2).astype(F32) * (two_pi / n2)
    eye = jnp.eye(k1l, dtype=F32)
    cb = jnp.einsum("kn,ab->kanb", jnp.cos(ang), eye).reshape(DFT_LONG, DFT_LONG)
    sb = jnp.einsum("kn,ab->kanb", jnp.sin(ang), eye).reshape(DFT_LONG, DFT_LONG)
    mb = jnp.concatenate([cb, -sb], axis=1).astype(BF16)
    return wc, wa, mb


def _fnet(x, mod, layer, b_off, norm_g, w_o, j):
    n_b, seq, _ = x.shape
    tm = min(TOKEN_TILE, seq)
    n2 = seq // DFT_LONG
    k1l = DFT_LONG // n2
    rows = tm // n2
    wc, wa, mb = _fnet_tables(seq)

    ab = pl.pallas_call(
        functools.partial(_fnet_chan_kernel, n2=n2),
        grid=(n_b, seq // tm),
        in_specs=[
            pl.BlockSpec((1, tm, D_MODEL), lambda b, i: (b, i, 0)),
            _mod_spec(layer, b_off),
            pl.BlockSpec((1, 1, D_MODEL), lambda b, i: (layer, 0, 0)),
            pl.BlockSpec((FNET_GROUP_DIM, 2 * FNET_GROUP_DIM), lambda b, i: (0, 0)),
        ],
        out_specs=pl.BlockSpec((1, 2, n2, rows, D_MODEL), lambda b, i: (b, 0, 0, i, 0)),
        out_shape=jax.ShapeDtypeStruct((n_b, 2, n2, DFT_LONG, D_MODEL), BF16),
        scratch_shapes=[pltpu.VMEM((2, D_MODEL // LANES, tm, LANES), F32)],
        compiler_params=_params(2),
        name="fnet_channel_dft",
    )(x, mod, norm_g, wc)

    v = pl.pallas_call(
        functools.partial(_fnet_stage_a_kernel, n2=n2),
        grid=(n_b, n2),
        in_specs=[
            pl.BlockSpec((1, 2, 1, DFT_LONG, D_MODEL), lambda b, i: (b, 0, i, 0, 0)),
            pl.BlockSpec((1, 2 * DFT_LONG, 2 * DFT_LONG), lambda b, i: (i, 0, 0)),
        ],
        out_specs=pl.BlockSpec((1, 2, n2, 1, k1l, D_MODEL), lambda b, i: (b, 0, 0, i, 0, 0)),
        out_shape=jax.ShapeDtypeStruct((n_b, 2, n2, n2, k1l, D_MODEL), BF16),
        compiler_params=_params(2),
        name="fnet_seq_stage_a",
    )(ab, wa)

    x5 = x.reshape(n_b, n2, n2, k1l, D_MODEL)
    out = pl.pallas_call(
        functools.partial(_fnet_stage_b_kernel, n2=n2, scale=1.0 / math.sqrt(seq)),
        grid=(n_b, n2),
        in_specs=[
            pl.BlockSpec((1, 2, 1, n2, k1l, D_MODEL), lambda b, i: (b, 0, i, 0, 0, 0)),
            pl.BlockSpec((1, n2, 1, k1l, D_MODEL), lambda b, i: (b, 0, i, 0, 0)),
            _mod_spec(layer, b_off),
            pl.BlockSpec((DFT_LONG, 2 * DFT_LONG), lambda b, i: (0, 0)),
            pl.BlockSpec((1, D_MODEL, D_MODEL), lambda b, i: (j, 0, 0)),
        ],
        out_specs=pl.BlockSpec((1, n2, 1, k1l, D_MODEL), lambda b, i: (b, 0, i, 0, 0)),
        out_shape=jax.ShapeDtypeStruct(x5.shape, F32),
        compiler_params=_params(2),
        name="fnet_seq_stage_b",
    )(v, x5, mod, mb, w_o)
    return out.reshape(x.shape)


def _attn_qkv_kernel(x_ref, mod_ref, g_ref, w_ref, qg_ref, kg_ref, cos_ref, sin_ref,
                     q_ref, k_ref, v_ref):
    x = x_ref[0]
    mod = mod_ref[0, 0]
    h = _norm_mod(x, g_ref[0], mod[0:1], mod[1:2]).astype(BF16)
    qkv = jnp.dot(h, w_ref[0], preferred_element_type=F32)
    cos = cos_ref[...]
    sin = sin_ref[...]
    quarter = HEAD_DIM // 4
    lane = lax.broadcasted_iota(jnp.int32, (x.shape[0], HEAD_DIM), 1)
    first = (lane % (2 * quarter)) < quarter

    def head(t, gain, scale):
        t = _rms(t) * gain
        partner = jnp.where(first, pltpu.roll(t, HEAD_DIM - quarter, 1),
                            pltpu.roll(t, quarter, 1))
        return ((t * cos + partner * sin) * scale).astype(BF16)

    hd = HEAD_DIM
    for i in range(N_HEADS):
        q_ref[0, :, i * hd:(i + 1) * hd] = head(qkv[:, i * hd:(i + 1) * hd], qg_ref[0],
                                                 HEAD_DIM ** -0.5)
    k0 = N_HEADS * hd
    for i in range(N_KV_HEADS):
        k_ref[0, :, i * hd:(i + 1) * hd] = head(qkv[:, k0 + i * hd:k0 + (i + 1) * hd],
                                                 kg_ref[0], 1.0)
    v_ref[0] = qkv[:, k0 + N_KV_HEADS * hd:].astype(BF16)


def _attn_kernel(q_ref, k_ref, v_ref, x_ref, mod_ref, wo_ref, o_ref, o_scr):
    hd = HEAD_DIM
    for g in range(N_KV_HEADS):
        k = k_ref[0, :, g * hd:(g + 1) * hd]
        v = v_ref[0, :, g * hd:(g + 1) * hd]
        for i in range(Q_PER_KV):
            hh = g * Q_PER_KV + i
            q = q_ref[0, :, hh * hd:(hh + 1) * hd]
            s = lax.dot_general(q, k, (((1,), (1,)), ((), ())), preferred_element_type=F32)
            p = jnp.exp(s - jnp.max(s, axis=-1, keepdims=True))
            denom = jnp.sum(p, axis=-1, keepdims=True)
            o = jnp.dot(p.astype(BF16), v, preferred_element_type=F32) / denom
            o_scr[:, hh * hd:(hh + 1) * hd] = o.astype(BF16)
    m = jnp.dot(o_scr[...], wo_ref[0], preferred_element_type=F32)
    o_ref[0] = x_ref[0] + mod_ref[0, 0][2:3] * m


def _rope_tables(seq):
    pos = jnp.arange(seq, dtype=jnp.int32)
    row = (pos // GRID_W).astype(F32)
    col = (pos % GRID_W).astype(F32)
    half = HEAD_DIM // 2
    freqs = 1.0 / (ROPE_THETA ** (jnp.arange(0, half, 2, dtype=F32) / half))
    ang_r = row[:, None] * freqs[None, :]
    ang_c = col[:, None] * freqs[None, :]
    cos = jnp.concatenate([jnp.cos(ang_r), jnp.cos(ang_r), jnp.cos(ang_c), jnp.cos(ang_c)], axis=1)
    sin = jnp.concatenate([-jnp.sin(ang_r), jnp.sin(ang_r), -jnp.sin(ang_c), jnp.sin(ang_c)], axis=1)
    return cos, sin


def _attention(x, mod, layer, b_off, norm_g, w_qkv, q_g, k_g, w_o, j):
    n_b, seq, _ = x.shape
    tm = min(TOKEN_TILE, seq)
    cos, sin = _rope_tables(seq)
    kv_dim = N_KV_HEADS * HEAD_DIM
    q, k, v = pl.pallas_call(
        _attn_qkv_kernel,
        grid=(n_b, seq // tm),
        in_specs=[
            pl.BlockSpec((1, tm, D_MODEL), lambda b, i: (b, i, 0)),
            _mod_spec(layer, b_off),
            pl.BlockSpec((1, 1, D_MODEL), lambda b, i: (layer, 0, 0)),
            pl.BlockSpec((1, D_MODEL, QKV_DIM), lambda b, i: (j, 0, 0)),
            pl.BlockSpec((1, 1, HEAD_DIM), lambda b, i: (j, 0, 0)),
            pl.BlockSpec((1, 1, HEAD_DIM), lambda b, i: (j, 0, 0)),
            pl.BlockSpec((tm, HEAD_DIM), lambda b, i: (i, 0)),
            pl.BlockSpec((tm, HEAD_DIM), lambda b, i: (i, 0)),
        ],
        out_specs=[
            pl.BlockSpec((1, tm, D_MODEL), lambda b, i: (b, i, 0)),
            pl.BlockSpec((1, tm, kv_dim), lambda b, i: (b, i, 0)),
            pl.BlockSpec((1, tm, kv_dim), lambda b, i: (b, i, 0)),
        ],
        out_shape=[
            jax.ShapeDtypeStruct((n_b, seq, D_MODEL), BF16),
            jax.ShapeDtypeStruct((n_b, seq, kv_dim), BF16),
            jax.ShapeDtypeStruct((n_b, seq, kv_dim), BF16),
        ],
        compiler_params=_params(2),
        name="attn_qkv_rope",
    )(x, mod, norm_g, w_qkv, q_g, k_g, cos, sin)

    tq = min(ATTN_Q_TILE, seq)
    return pl.pallas_call(
        _attn_kernel,
        grid=(n_b, seq // tq),
        in_specs=[
            pl.BlockSpec((1, tq, D_MODEL), lambda b, i: (b, i, 0)),
            pl.BlockSpec((1, seq, kv_dim), lambda b, i: (b, 0, 0)),
            pl.BlockSpec((1, seq, kv_dim), lambda b, i: (b, 0, 0)),
            pl.BlockSpec((1, tq, D_MODEL), lambda b, i: (b, i, 0)),
            _mod_spec(layer, b_off),
            pl.BlockSpec((1, D_MODEL, D_MODEL), lambda b, i: (j, 0, 0)),
        ],
        out_specs=pl.BlockSpec((1, tq, D_MODEL), lambda b, i: (b, i, 0)),
        out_shape=jax.ShapeDtypeStruct(x.shape, F32),
        scratch_shapes=[pltpu.VMEM((tq, D_MODEL), BF16)],
        compiler_params=_params(2),
        name="attn_softmax_out",
    )(q, k, v, x, mod, w_o)


def _sgu_kernel(x_ref, mod_ref, g_ref, win_ref, lng_ref, lnb_ref, ws_ref, bias_ref, wo_ref,
                o_ref, y_scr):
    x = x_ref[0]
    mod = mod_ref[0, 0]
    h = _norm_mod(x, g_ref[0], mod[0:1], mod[1:2]).astype(BF16)
    uv = jnp.dot(h, win_ref[0], preferred_element_type=F32)
    uv = 0.5 * uv * (1.0 + lax.erf(uv * math.sqrt(0.5)))
    u = uv[:, :D_MODEL]
    v = uv[:, D_MODEL:]
    vc = v - jnp.mean(v, axis=-1, keepdims=True)
    vn = vc * lax.rsqrt(jnp.mean(vc * vc, axis=-1, keepdims=True) + EPS)
    vn = (vn * lng_ref[0] + lnb_ref[0]).astype(BF16)
    ck = SGU_CHUNK
    gd = D_MODEL // SGU_GROUPS
    bias = bias_ref[0]
    for c in range(x.shape[0] // ck):
        for g in range(SGU_GROUPS):
            sv = jnp.dot(ws_ref[0, g], vn[c * ck:(c + 1) * ck, g * gd:(g + 1) * gd],
                         preferred_element_type=F32) + bias[:, g * gd:(g + 1) * gd]
            y_scr[c * ck:(c + 1) * ck, g * gd:(g + 1) * gd] = (
                u[c * ck:(c + 1) * ck, g * gd:(g + 1) * gd] * sv).astype(BF16)
    m = jnp.dot(y_scr[...], wo_ref[0], preferred_element_type=F32)
    o_ref[0] = x + mod[2:3] * m


def _sgu(x, mod, layer, b_off, norm_g, w_in, ln_g, ln_b, w_s, bias, w_o, j):
    n_b, seq, _ = x.shape
    tm = min(TOKEN_TILE, seq)
    return pl.pallas_call(
        _sgu_kernel,
        grid=(n_b, seq // tm),
        in_specs=[
            pl.BlockSpec((1, tm, D_MODEL), lambda b, i: (b, i, 0)),
            _mod_spec(layer, b_off),
            pl.BlockSpec((1, 1, D_MODEL), lambda b, i: (layer, 0, 0)),
            pl.BlockSpec((1, D_MODEL, 2 * D_MODEL), lambda b, i: (j, 0, 0)),
            pl.BlockSpec((1, 1, D_MODEL), lambda b, i: (j, 0, 0)),
            pl.BlockSpec((1, 1, D_MODEL), lambda b, i: (j, 0, 0)),
            pl.BlockSpec((1, SGU_GROUPS, SGU_CHUNK, SGU_CHUNK), lambda b, i: (j, 0, 0, 0)),
            pl.BlockSpec((1, SGU_CHUNK, D_MODEL), lambda b, i: (j, 0, 0)),
            pl.BlockSpec((1, D_MODEL, D_MODEL), lambda b, i: (j, 0, 0)),
        ],
        out_specs=pl.BlockSpec((1, tm, D_MODEL), lambda b, i: (b, i, 0)),
        out_shape=jax.ShapeDtypeStruct(x.shape, F32),
        scratch_shapes=[pltpu.VMEM((tm, D_MODEL), BF16)],
        compiler_params=_params(2),
        name="sgu",
    )(x, mod, norm_g, w_in, ln_g, ln_b, w_s, bias, w_o)


def _trunk(x, mod, b_off, w):
    for i in range(DEPTH):
        kind, j = i % 3, i // 3
        if kind == 0:
            x = _fnet(x, mod, i, b_off, w["norm1_g"], w["fnet_w_o"], j)
        elif kind == 1:
            x = _attention(x, mod, i, b_off, w["norm1_g"], w["attn_w_qkv"], w["attn_q_g"],
                           w["attn_k_g"], w["attn_w_o"], j)
        else:
            x = _sgu(x, mod, i, b_off, w["norm1_g"], w["sgu_w_in"], w["sgu_ln_g"], w["sgu_ln_b"],
                     w["sgu_w_s"], w["sgu_bias"], w["sgu_w_o"], j)
        x = _ffn(x, mod, i, b_off, w["norm2_g"], w["ffn_wgu"], w["ffn_wd"], w["final_g"],
                 final=(i == DEPTH - 1))
    return x


def kernel(x_prompt, x_sample, c_prompt, c_sample, norm1_g, norm2_g, w_ada, b_ada, fnet_w_o,
           attn_w_qkv, attn_q_g, attn_k_g, attn_w_o, sgu_w_in, sgu_ln_g, sgu_ln_b, sgu_w_s, sgu_b_s,
           sgu_w_o, ffn_w_gu, ffn_w_down, final_g):
    depth = norm1_g.shape[0]
    hidden = ffn_w_down.shape[1]
    n_chunks = hidden // FFN_CHUNK
    wg = ffn_w_gu[:, :, :hidden].reshape(depth, D_MODEL, n_chunks, FFN_CHUNK)
    wu = ffn_w_gu[:, :, hidden:].reshape(depth, D_MODEL, n_chunks, FFN_CHUNK)
    wgu = jnp.concatenate([wg, wu], axis=-1).transpose(0, 2, 1, 3).astype(BF16)
    wd = ffn_w_down.reshape(depth, n_chunks, FFN_CHUNK, D_MODEL).astype(BF16)
    gd = D_MODEL // SGU_GROUPS
    sgu_bias = jnp.repeat(jnp.swapaxes(sgu_b_s, 1, 2), gd, axis=2)
    w = {
        "norm1_g": norm1_g.reshape(depth, 1, D_MODEL),
        "norm2_g": norm2_g.reshape(depth, 1, D_MODEL),
        "fnet_w_o": fnet_w_o.astype(BF16),
        "attn_w_qkv": attn_w_qkv.astype(BF16),
        "attn_q_g": attn_q_g.reshape(-1, 1, HEAD_DIM),
        "attn_k_g": attn_k_g.reshape(-1, 1, HEAD_DIM),
        "attn_w_o": attn_w_o.astype(BF16),
        "sgu_w_in": sgu_w_in.astype(BF16),
        "sgu_ln_g": sgu_ln_g.reshape(-1, 1, D_MODEL),
        "sgu_ln_b": sgu_ln_b.reshape(-1, 1, D_MODEL),
        "sgu_w_s": sgu_w_s.astype(BF16),
        "sgu_bias": sgu_bias,
        "sgu_w_o": sgu_w_o.astype(BF16),
        "ffn_wgu": wgu,
        "ffn_wd": wd,
        "final_g": final_g.reshape(1, D_MODEL),
    }
    c_all = jnp.concatenate([c_prompt, c_sample], axis=0)
    mod = _modulation(c_all, w_ada, b_ada)
    y_prompt = _trunk(x_prompt, mod, 0, w)
    y_sample = _trunk(x_sample, mod, x_prompt.shape[0], w)
    return (y_prompt, y_sample)
```

```python
import functools
import math

import jax
import jax.numpy as jnp
from jax import lax
from jax.experimental import pallas as pl
from jax.experimental.pallas import tpu as pltpu

F32 = jnp.float32
BF16 = jnp.bfloat16

D_MODEL = 1024
DEPTH = 4
EPS = 1e-6
GRID_W = 64
FNET_GROUPS = 4
FNET_GROUP_DIM = D_MODEL // FNET_GROUPS
HEAD_DIM = 128
N_HEADS = 8
N_KV_HEADS = 2
Q_PER_KV = N_HEADS // N_KV_HEADS
QKV_DIM = (N_HEADS + 2 * N_KV_HEADS) * HEAD_DIM
ROPE_THETA = 10000.0
SGU_GROUPS = 8
SGU_CHUNK = 128
FFN_HIDDEN = 2816

DFT_LONG = 256
LANES = 128
TOKEN_TILE = 512
FFN_TOKEN_TILE = 1024
LOG2_E = 1.4426950408889634
ATTN_Q_TILE = 256
FFN_CHUNK = 256
VMEM_LIMIT = 56 * 1024 * 1024


def _params(n_axes):
    return pltpu.CompilerParams(
        dimension_semantics=("parallel",) * n_axes, vmem_limit_bytes=VMEM_LIMIT)


def _resident(block_shape, index_map):
    return pl.BlockSpec(block_shape, index_map, pipeline_mode=pl.Buffered(1))


def _rms(x):
    return x * lax.rsqrt(jnp.mean(x * x, axis=-1, keepdims=True) + EPS)


def _norm_mod(x, gain, shift, scale):
    return _rms(x) * (gain * (1.0 + scale)) + shift


def _mod_kernel(c_ref, w_ref, b_ref, o_ref):
    c = c_ref[...]
    cs = c * jax.nn.sigmoid(c)
    o_ref[0] = jnp.dot(cs, w_ref[0], precision=lax.Precision.HIGHEST,
                       preferred_element_type=F32) + b_ref[0]


def _modulation(c_all, w_ada, b_ada):
    n_b = c_all.shape[0]
    tn = 1536
    out = pl.pallas_call(
        _mod_kernel,
        grid=(DEPTH, 6 * D_MODEL // tn),
        in_specs=[
            pl.BlockSpec((n_b, D_MODEL), lambda l, j: (0, 0)),
            pl.BlockSpec((1, D_MODEL, tn), lambda l, j: (l, 0, j)),
            pl.BlockSpec((1, 1, tn), lambda l, j: (l, 0, j)),
        ],
        out_specs=pl.BlockSpec((1, n_b, tn), lambda l, j: (l, 0, j)),
        out_shape=jax.ShapeDtypeStruct((DEPTH, n_b, 6 * D_MODEL), F32),
        compiler_params=_params(2),
        name="adaln_mod",
    )(c_all, w_ada, b_ada.reshape(DEPTH, 1, 6 * D_MODEL))
    return out.reshape(DEPTH, n_b, 6, D_MODEL)


def _mod_spec(layer, b_off):
    return pl.BlockSpec((1, 1, 6, D_MODEL), lambda b, i: (layer, b_off + b, 0, 0))


def _ffn_kernel(x_ref, mod_ref, g_ref, wgu_ref, wd_ref, fg_ref, o_ref, h_scr, acc_scr,
                *, n_chunks, chunk, final):
    x = x_ref[0]
    mod = mod_ref[0, 0]
    h_scr[...] = _norm_mod(x, g_ref[0], mod[3:4], mod[4:5]).astype(BF16)
    acc_scr[...] = jnp.zeros_like(acc_scr)

    for j in range(n_chunks):
        gu = jnp.dot(h_scr[...], wgu_ref[0, j], preferred_element_type=F32)
        g = gu[:, :chunk]
        u = gu[:, chunk:]
        a = (g * jax.nn.sigmoid(g) * u).astype(BF16)
        acc_scr[...] += jnp.dot(a, wd_ref[0, j], preferred_element_type=F32)
    y = x + mod[5:6] * acc_scr[...]
    if final:
        y = _rms(y) * fg_ref[...]
    o_ref[0] = y


def _ffn(x, mod, layer, b_off, norm_g, wgu, wd, final_g, final):
    n_b, seq, _ = x.shape
    tm = min(FFN_TOKEN_TILE, seq)
    n_chunks = wgu.shape[1]
    chunk = wd.shape[2]
    kern = functools.partial(_ffn_kernel, n_chunks=n_chunks, chunk=chunk, final=final)
    return pl.pallas_call(
        kern,
        grid=(n_b, seq // tm),
        in_specs=[
            pl.BlockSpec((1, tm, D_MODEL), lambda b, i: (b, i, 0)),
            _mod_spec(layer, b_off),
            pl.BlockSpec((1, 1, D_MODEL), lambda b, i: (layer, 0, 0)),
            _resident((1, n_chunks, D_MODEL, 2 * chunk), lambda b, i: (layer, 0, 0, 0)),
            _resident((1, n_chunks, chunk, D_MODEL), lambda b, i: (layer, 0, 0, 0)),
            pl.BlockSpec((1, D_MODEL), lambda b, i: (0, 0)),
        ],
        out_specs=pl.BlockSpec((1, tm, D_MODEL), lambda b, i: (b, i, 0)),
        out_shape=jax.ShapeDtypeStruct(x.shape, F32),
        scratch_shapes=[pltpu.VMEM((tm, D_MODEL), BF16), pltpu.VMEM((tm, D_MODEL), F32)],
        compiler_params=_params(2),
        name="ffn_final" if final else "ffn",
    )(x, mod, norm_g, wgu, wd, final_g)


def _fnet_chan_kernel(x_ref, mod_ref, g_ref, wc_ref, o_ref, ab_scr, *, n2):
    x = x_ref[0]
    mod = mod_ref[0, 0]
    h = _norm_mod(x, g_ref[0], mod[0:1], mod[1:2]).astype(BF16)
    gd = FNET_GROUP_DIM
    per_group = gd // LANES
    for g in range(FNET_GROUPS):
        ab = jnp.dot(h[:, g * gd:(g + 1) * gd], wc_ref[...], preferred_element_type=F32)
        for part in range(2):
            for t in range(per_group):
                ab_scr[part, g * per_group + t] = ab[:, part * gd + t * LANES:part * gd + (t + 1) * LANES]
    rows = x.shape[0] // n2
    for j in range(n2):
        for part in range(2):
            for blk in range(D_MODEL // LANES):
                sel = ab_scr[part, blk, pl.ds(j, rows, stride=n2), :]
                o_ref[0, part, j, :, blk * LANES:(blk + 1) * LANES] = sel.astype(BF16)


def _fnet_stage_a_kernel(ab_ref, wa_ref, o_ref, *, n2):
    ab = ab_ref[0, :, 0].reshape(2 * DFT_LONG, D_MODEL)
    v = jnp.dot(wa_ref[0], ab, preferred_element_type=F32)
    o_ref[0, :, :, 0] = v.reshape(2, n2, DFT_LONG // n2, D_MODEL).astype(BF16)


def _fnet_stage_b_kernel(v_ref, x_ref, mod_ref, mb_ref, wo_ref, o_ref, *, n2, scale):
    k1l = DFT_LONG // n2
    vb = v_ref[0, :, 0].reshape(2 * DFT_LONG, D_MODEL)
    f = jnp.dot(mb_ref[...], vb, preferred_element_type=F32) * scale
    m = jnp.dot(f.astype(BF16), wo_ref[0], preferred_element_type=F32)
    x = x_ref[0, :, 0].reshape(DFT_LONG, D_MODEL)
    mod = mod_ref[0, 0]
    o_ref[0, :, 0] = (x + mod[2:3] * m).reshape(n2, k1l, D_MODEL)


def _fnet_tables(seq):
    n2 = seq // DFT_LONG
    k1l = DFT_LONG // n2
    two_pi = 2.0 * math.pi
    gd = FNET_GROUP_DIM
    c = jnp.arange(gd, dtype=jnp.int32)
    ang = ((c[:, None] * c[None, :]) % gd).astype(F32) * (two_pi / gd)
    inv = 1.0 / math.sqrt(gd)
    wc = jnp.concatenate([jnp.cos(ang) * inv, jnp.sin(ang) * inv], axis=1).astype(BF16)

    k1 = jnp.arange(DFT_LONG, dtype=jnp.int32)
    n1 = jnp.arange(DFT_LONG, dtype=jnp.int32)
    j2 = jnp.arange(n2, dtype=jnp.int32)
    sp = n2 * n1[None, None, :] + j2[:, None, None]
    ang = ((k1[None, :, None] * sp) % seq).astype(F32) * (two_pi / seq)
    ca, sa = jnp.cos(ang), jnp.sin(ang)
    wa = jnp.concatenate([jnp.concatenate([ca, -sa], axis=2),
                          jnp.concatenate([sa, ca], axis=2)], axis=1).astype(BF16)

    ang = ((j2[:, None] * j2[None, :]) % n2).astype(F32) * (two_pi / n2)
    eye = jnp.eye(k1l, dtype=F32)
    cb = jnp.einsum("kn,ab->kanb", jnp.cos(ang), eye).reshape(DFT_LONG, DFT_LONG)
    sb = jnp.einsum("kn,ab->kanb", jnp.sin(ang), eye).reshape(DFT_LONG, DFT_LONG)
    mb = jnp.concatenate([cb, -sb], axis=1).astype(BF16)
    return wc, wa, mb


def _fnet(x, mod, layer, b_off, norm_g, w_o, j):
    n_b, seq, _ = x.shape
    tm = min(TOKEN_TILE, seq)
    n2 = seq // DFT_LONG
    k1l = DFT_LONG // n2
    rows = tm // n2
    wc, wa, mb = _fnet_tables(seq)

    ab = pl.pallas_call(
        functools.partial(_fnet_chan_kernel, n2=n2),
        grid=(n_b, seq // tm),
        in_specs=[
            pl.BlockSpec((1, tm, D_MODEL), lambda b, i: (b, i, 0)),
            _mod_spec(layer, b_off),
            pl.BlockSpec((1, 1, D_MODEL), lambda b, i: (layer, 0, 0)),
            pl.BlockSpec((FNET_GROUP_DIM, 2 * FNET_GROUP_DIM), lambda b, i: (0, 0)),
        ],
        out_specs=pl.BlockSpec((1, 2, n2, rows, D_MODEL), lambda b, i: (b, 0, 0, i, 0)),
        out_shape=jax.ShapeDtypeStruct((n_b, 2, n2, DFT_LONG, D_MODEL), BF16),
        scratch_shapes=[pltpu.VMEM((2, D_MODEL // LANES, tm, LANES), F32)],
        compiler_params=_params(2),
        name="fnet_channel_dft",
    )(x, mod, norm_g, wc)

    v = pl.pallas_call(
        functools.partial(_fnet_stage_a_kernel, n2=n2),
        grid=(n_b, n2),
        in_specs=[
            pl.BlockSpec((1, 2, 1, DFT_LONG, D_MODEL), lambda b, i: (b, 0, i, 0, 0)),
            pl.BlockSpec((1, 2 * DFT_LONG, 2 * DFT_LONG), lambda b, i: (i, 0, 0)),
        ],
        out_specs=pl.BlockSpec((1, 2, n2, 1, k1l, D_MODEL), lambda b, i: (b, 0, 0, i, 0, 0)),
        out_shape=jax.ShapeDtypeStruct((n_b, 2, n2, n2, k1l, D_MODEL), BF16),
        compiler_params=_params(2),
        name="fnet_seq_stage_a",
    )(ab, wa)

    x5 = x.reshape(n_b, n2, n2, k1l, D_MODEL)
    out = pl.pallas_call(
        functools.partial(_fnet_stage_b_kernel, n2=n2, scale=1.0 / math.sqrt(seq)),
        grid=(n_b, n2),
        in_specs=[
            pl.BlockSpec((1, 2, 1, n2, k1l, D_MODEL), lambda b, i: (b, 0, i, 0, 0, 0)),
            pl.BlockSpec((1, n2, 1, k1l, D_MODEL), lambda b, i: (b, 0, i, 0, 0)),
            _mod_spec(layer, b_off),
            pl.BlockSpec((DFT_LONG, 2 * DFT_LONG), lambda b, i: (0, 0)),
            pl.BlockSpec((1, D_MODEL, D_MODEL), lambda b, i: (j, 0, 0)),
        ],
        out_specs=pl.BlockSpec((1, n2, 1, k1l, D_MODEL), lambda b, i: (b, 0, i, 0, 0)),
        out_shape=jax.ShapeDtypeStruct(x5.shape, F32),
        compiler_params=_params(2),
        name="fnet_seq_stage_b",
    )(v, x5, mod, mb, w_o)
    return out.reshape(x.shape)


def _attn_qkv_kernel(x_ref, mod_ref, g_ref, w_ref, qg_ref, kg_ref, cos_ref, sin_ref,
                     q_ref, k_ref, v_ref):
    x = x_ref[0]
    mod = mod_ref[0, 0]
    h = _norm_mod(x, g_ref[0], mod[0:1], mod[1:2]).astype(BF16)
    qkv = jnp.dot(h, w_ref[0], preferred_element_type=F32)
    cos = cos_ref[...]
    sin = sin_ref[...]
    quarter = HEAD_DIM // 4
    lane = lax.broadcasted_iota(jnp.int32, (x.shape[0], HEAD_DIM), 1)
    first = (lane % (2 * quarter)) < quarter

    def head(t, gain, scale):
        t = _rms(t) * gain
        partner = jnp.where(first, pltpu.roll(t, HEAD_DIM - quarter, 1),
                            pltpu.roll(t, quarter, 1))
        return ((t * cos + partner * sin) * scale).astype(BF16)

    hd = HEAD_DIM
    for i in range(N_HEADS):
        q_ref[0, :, i * hd:(i + 1) * hd] = head(qkv[:, i * hd:(i + 1) * hd], qg_ref[0],
                                                 HEAD_DIM ** -0.5 * LOG2_E)
    k0 = N_HEADS * hd
    v0 = k0 + N_KV_HEADS * hd
    ones = jnp.ones((x.shape[0], hd), BF16)
    for i in range(N_KV_HEADS):
        k_ref[0, :, i * hd:(i + 1) * hd] = head(qkv[:, k0 + i * hd:k0 + (i + 1) * hd],
                                                 kg_ref[0], 1.0)
        v_ref[0, :, 2 * i * hd:(2 * i + 1) * hd] = qkv[:, v0 + i * hd:v0 + (i + 1) * hd].astype(BF16)
        v_ref[0, :, (2 * i + 1) * hd:(2 * i + 2) * hd] = ones


def _attn_kernel(q_ref, k_ref, v_ref, x_ref, mod_ref, wo_ref, o_ref, o_scr):
    hd = HEAD_DIM
    for g in range(N_KV_HEADS):
        k = k_ref[0, :, g * hd:(g + 1) * hd]
        v1 = v_ref[0, :, 2 * g * hd:(2 * g + 2) * hd]
        for i in range(Q_PER_KV):
            hh = g * Q_PER_KV + i
            q = q_ref[0, :, hh * hd:(hh + 1) * hd]
            s = lax.dot_general(q, k, (((1,), (1,)), ((), ())), preferred_element_type=F32)
            p = jnp.exp2(s - jnp.max(s, axis=-1, keepdims=True)).astype(BF16)
            pv = jnp.dot(p, v1, preferred_element_type=F32)
            o = pv[:, :hd] / pv[:, hd:hd + 1]
            o_scr[:, hh * hd:(hh + 1) * hd] = o.astype(BF16)
    m = jnp.dot(o_scr[...], wo_ref[0], preferred_element_type=F32)
    o_ref[0] = x_ref[0] + mod_ref[0, 0][2:3] * m


def _rope_tables(seq):
    pos = jnp.arange(seq, dtype=jnp.int32)
    row = (pos // GRID_W).astype(F32)
    col = (pos % GRID_W).astype(F32)
    half = HEAD_DIM // 2
    freqs = 1.0 / (ROPE_THETA ** (jnp.arange(0, half, 2, dtype=F32) / half))
    ang_r = row[:, None] * freqs[None, :]
    ang_c = col[:, None] * freqs[None, :]
    cos = jnp.concatenate([jnp.cos(ang_r), jnp.cos(ang_r), jnp.cos(ang_c), jnp.cos(ang_c)], axis=1)
    sin = jnp.concatenate([-jnp.sin(ang_r), jnp.sin(ang_r), -jnp.sin(ang_c), jnp.sin(ang_c)], axis=1)
    return cos, sin


def _attention(x, mod, layer, b_off, norm_g, w_qkv, q_g, k_g, w_o, j):
    n_b, seq, _ = x.shape
    tm = min(TOKEN_TILE, seq)
    cos, sin = _rope_tables(seq)
    kv_dim = N_KV_HEADS * HEAD_DIM
    q, k, v = pl.pallas_call(
        _attn_qkv_kernel,
        grid=(n_b, seq // tm),
        in_specs=[
            pl.BlockSpec((1, tm, D_MODEL), lambda b, i: (b, i, 0)),
            _mod_spec(layer, b_off),
            pl.BlockSpec((1, 1, D_MODEL), lambda b, i: (layer, 0, 0)),
            pl.BlockSpec((1, D_MODEL, QKV_DIM), lambda b, i: (j, 0, 0)),
            pl.BlockSpec((1, 1, HEAD_DIM), lambda b, i: (j, 0, 0)),
            pl.BlockSpec((1, 1, HEAD_DIM), lambda b, i: (j, 0, 0)),
            pl.BlockSpec((tm, HEAD_DIM), lambda b, i: (i, 0)),
            pl.BlockSpec((tm, HEAD_DIM), lambda b, i: (i, 0)),
        ],
        out_specs=[
            pl.BlockSpec((1, tm, D_MODEL), lambda b, i: (b, i, 0)),
            pl.BlockSpec((1, tm, kv_dim), lambda b, i: (b, i, 0)),
            pl.BlockSpec((1, tm, 2 * kv_dim), lambda b, i: (b, i, 0)),
        ],
        out_shape=[
            jax.ShapeDtypeStruct((n_b, seq, D_MODEL), BF16),
            jax.ShapeDtypeStruct((n_b, seq, kv_dim), BF16),
            jax.ShapeDtypeStruct((n_b, seq, 2 * kv_dim), BF16),
        ],
        compiler_params=_params(2),
        name="attn_qkv_rope",
    )(x, mod, norm_g, w_qkv, q_g, k_g, cos, sin)

    tq = min(ATTN_Q_TILE, seq)
    return pl.pallas_call(
        _attn_kernel,
        grid=(n_b, seq // tq),
        in_specs=[
            pl.BlockSpec((1, tq, D_MODEL), lambda b, i: (b, i, 0)),
            pl.BlockSpec((1, seq, kv_dim), lambda b, i: (b, 0, 0)),
            pl.BlockSpec((1, seq, 2 * kv_dim), lambda b, i: (b, 0, 0)),
            pl.BlockSpec((1, tq, D_MODEL), lambda b, i: (b, i, 0)),
            _mod_spec(layer, b_off),
            pl.BlockSpec((1, D_MODEL, D_MODEL), lambda b, i: (j, 0, 0)),
        ],
        out_specs=pl.BlockSpec((1, tq, D_MODEL), lambda b, i: (b, i, 0)),
        out_shape=jax.ShapeDtypeStruct(x.shape, F32),
        scratch_shapes=[pltpu.VMEM((tq, D_MODEL), BF16)],
        compiler_params=_params(2),
        name="attn_softmax_out",
    )(q, k, v, x, mod, w_o)


def _sgu_kernel(x_ref, mod_ref, g_ref, win_ref, lng_ref, lnb_ref, ws_ref, bias_ref, wo_ref,
                o_ref, y_scr):
    x = x_ref[0]
    mod = mod_ref[0, 0]
    h = _norm_mod(x, g_ref[0], mod[0:1], mod[1:2]).astype(BF16)
    uv = jnp.dot(h, win_ref[0], preferred_element_type=F32)
    uv = 0.5 * uv * (1.0 + lax.erf(uv * math.sqrt(0.5)))
    u = uv[:, :D_MODEL]
    v = uv[:, D_MODEL:]
    vc = v - jnp.mean(v, axis=-1, keepdims=True)
    vn = vc * lax.rsqrt(jnp.mean(vc * vc, axis=-1, keepdims=True) + EPS)
    vn = (vn * lng_ref[0] + lnb_ref[0]).astype(BF16)
    ck = SGU_CHUNK
    gd = D_MODEL // SGU_GROUPS
    bias = bias_ref[0]
    for c in range(x.shape[0] // ck):
        for g in range(SGU_GROUPS):
            sv = jnp.dot(ws_ref[0, g], vn[c * ck:(c + 1) * ck, g * gd:(g + 1) * gd],
                         preferred_element_type=F32) + bias[:, g * gd:(g + 1) * gd]
            y_scr[c * ck:(c + 1) * ck, g * gd:(g + 1) * gd] = (
                u[c * ck:(c + 1) * ck, g * gd:(g + 1) * gd] * sv).astype(BF16)
    m = jnp.dot(y_scr[...], wo_ref[0], preferred_element_type=F32)
    o_ref[0] = x + mod[2:3] * m


def _sgu(x, mod, layer, b_off, norm_g, w_in, ln_g, ln_b, w_s, bias, w_o, j):
    n_b, seq, _ = x.shape
    tm = min(TOKEN_TILE, seq)
    return pl.pallas_call(
        _sgu_kernel,
        grid=(n_b, seq // tm),
        in_specs=[
            pl.BlockSpec((1, tm, D_MODEL), lambda b, i: (b, i, 0)),
            _mod_spec(layer, b_off),
            pl.BlockSpec((1, 1, D_MODEL), lambda b, i: (layer, 0, 0)),
            pl.BlockSpec((1, D_MODEL, 2 * D_MODEL), lambda b, i: (j, 0, 0)),
            pl.BlockSpec((1, 1, D_MODEL), lambda b, i: (j, 0, 0)),
            pl.BlockSpec((1, 1, D_MODEL), lambda b, i: (j, 0, 0)),
            pl.BlockSpec((1, SGU_GROUPS, SGU_CHUNK, SGU_CHUNK), lambda b, i: (j, 0, 0, 0)),
            pl.BlockSpec((1, SGU_CHUNK, D_MODEL), lambda b, i: (j, 0, 0)),
            pl.BlockSpec((1, D_MODEL, D_MODEL), lambda b, i: (j, 0, 0)),
        ],
        out_specs=pl.BlockSpec((1, tm, D_MODEL), lambda b, i: (b, i, 0)),
        out_shape=jax.ShapeDtypeStruct(x.shape, F32),
        scratch_shapes=[pltpu.VMEM((tm, D_MODEL), BF16)],
        compiler_params=_params(2),
        name="sgu",
    )(x, mod, norm_g, w_in, ln_g, ln_b, w_s, bias, w_o)


def _trunk(x, mod, b_off, w):
    for i in range(DEPTH):
        kind, j = i % 3, i // 3
        if kind == 0:
            x = _fnet(x, mod, i, b_off, w["norm1_g"], w["fnet_w_o"], j)
        elif kind == 1:
            x = _attention(x, mod, i, b_off, w["norm1_g"], w["attn_w_qkv"], w["attn_q_g"],
                           w["attn_k_g"], w["attn_w_o"], j)
        else:
            x = _sgu(x, mod, i, b_off, w["norm1_g"], w["sgu_w_in"], w["sgu_ln_g"], w["sgu_ln_b"],
                     w["sgu_w_s"], w["sgu_bias"], w["sgu_w_o"], j)
        x = _ffn(x, mod, i, b_off, w["norm2_g"], w["ffn_wgu"], w["ffn_wd"], w["final_g"],
                 final=(i == DEPTH - 1))
    return x


def kernel(x_prompt, x_sample, c_prompt, c_sample, norm1_g, norm2_g, w_ada, b_ada, fnet_w_o,
           attn_w_qkv, attn_q_g, attn_k_g, attn_w_o, sgu_w_in, sgu_ln_g, sgu_ln_b, sgu_w_s, sgu_b_s,
           sgu_w_o, ffn_w_gu, ffn_w_down, final_g):
    depth = norm1_g.shape[0]
    hidden = ffn_w_down.shape[1]
    n_chunks = hidden // FFN_CHUNK
    wg = ffn_w_gu[:, :, :hidden].reshape(depth, D_MODEL, n_chunks, FFN_CHUNK)
    wu = ffn_w_gu[:, :, hidden:].reshape(depth, D_MODEL, n_chunks, FFN_CHUNK)
    wgu = jnp.concatenate([wg, wu], axis=-1).transpose(0, 2, 1, 3).astype(BF16)
    wd = ffn_w_down.reshape(depth, n_chunks, FFN_CHUNK, D_MODEL).astype(BF16)
    gd = D_MODEL // SGU_GROUPS
    sgu_bias = jnp.repeat(jnp.swapaxes(sgu_b_s, 1, 2), gd, axis=2)
    w = {
        "norm1_g": norm1_g.reshape(depth, 1, D_MODEL),
        "norm2_g": norm2_g.reshape(depth, 1, D_MODEL),
        "fnet_w_o": fnet_w_o.astype(BF16),
        "attn_w_qkv": attn_w_qkv.astype(BF16),
        "attn_q_g": attn_q_g.reshape(-1, 1, HEAD_DIM),
        "attn_k_g": attn_k_g.reshape(-1, 1, HEAD_DIM),
        "attn_w_o": attn_w_o.astype(BF16),
        "sgu_w_in": sgu_w_in.astype(BF16),
        "sgu_ln_g": sgu_ln_g.reshape(-1, 1, D_MODEL),
        "sgu_ln_b": sgu_ln_b.reshape(-1, 1, D_MODEL),
        "sgu_w_s": sgu_w_s.astype(BF16),
        "sgu_bias": sgu_bias,
        "sgu_w_o": sgu_w_o.astype(BF16),
        "ffn_wgu": wgu,
        "ffn_wd": wd,
        "final_g": final_g.reshape(1, D_MODEL),
    }
    c_all = jnp.concatenate([c_prompt, c_sample], axis=0)
    mod = _modulation(c_all, w_ada, b_ada)
    y_prompt = _trunk(x_prompt, mod, 0, w)
    y_sample = _trunk(x_sample, mod, x_prompt.shape[0], w)
    return (y_prompt, y_sample)
```

```python
import functools
import math

import jax
import jax.numpy as jnp
from jax import lax
from jax.experimental import pallas as pl
from jax.experimental.pallas import tpu as pltpu

F32 = jnp.float32
BF16 = jnp.bfloat16

D_MODEL = 1024
DEPTH = 4
EPS = 1e-6
GRID_W = 64
FNET_GROUPS = 4
FNET_GROUP_DIM = D_MODEL // FNET_GROUPS
HEAD_DIM = 128
N_HEADS = 8
N_KV_HEADS = 2
Q_PER_KV = N_HEADS // N_KV_HEADS
QKV_DIM = (N_HEADS + 2 * N_KV_HEADS) * HEAD_DIM
ROPE_THETA = 10000.0
SGU_GROUPS = 8
SGU_CHUNK = 128
FFN_HIDDEN = 2816

DFT_LONG = 256
LANES = 128
TOKEN_TILE = 512
FFN_TOKEN_TILE = 1024
LOG2_E = 1.4426950408889634
ATTN_Q_TILE = 256
FFN_CHUNK = 256
VMEM_LIMIT = 56 * 1024 * 1024


def _params(n_axes):
    return pltpu.CompilerParams(
        dimension_semantics=("parallel",) * n_axes, vmem_limit_bytes=VMEM_LIMIT)


def _resident(block_shape, index_map):
    return pl.BlockSpec(block_shape, index_map, pipeline_mode=pl.Buffered(1))


def _rms(x):
    return x * lax.rsqrt(jnp.mean(x * x, axis=-1, keepdims=True) + EPS)


def _norm_mod(x, gain, shift, scale):
    return _rms(x) * (gain * (1.0 + scale)) + shift


def _mod_kernel(c_ref, w_ref, b_ref, o_ref):
    c = c_ref[...]
    cs = c * jax.nn.sigmoid(c)
    o_ref[0] = jnp.dot(cs, w_ref[0], precision=lax.Precision.HIGHEST,
                       preferred_element_type=F32) + b_ref[0]


def _modulation(c_all, w_ada, b_ada):
    n_b = c_all.shape[0]
    tn = 1536
    out = pl.pallas_call(
        _mod_kernel,
        grid=(DEPTH, 6 * D_MODEL // tn),
        in_specs=[
            pl.BlockSpec((n_b, D_MODEL), lambda l, j: (0, 0)),
            pl.BlockSpec((1, D_MODEL, tn), lambda l, j: (l, 0, j)),
            pl.BlockSpec((1, 1, tn), lambda l, j: (l, 0, j)),
        ],
        out_specs=pl.BlockSpec((1, n_b, tn), lambda l, j: (l, 0, j)),
        out_shape=jax.ShapeDtypeStruct((DEPTH, n_b, 6 * D_MODEL), F32),
        compiler_params=_params(2),
        name="adaln_mod",
    )(c_all, w_ada, b_ada.reshape(DEPTH, 1, 6 * D_MODEL))
    return out.reshape(DEPTH, n_b, 6, D_MODEL)


def _mod_spec(layer, b_off):
    return pl.BlockSpec((1, 1, 6, D_MODEL), lambda b, i: (layer, b_off + b, 0, 0))


def _ffn_kernel(*refs, n_chunks, chunk, final, proj, emit_h):
    refs = list(refs)
    x_ref, mod_ref, g_ref, wgu_ref, wd_ref, fg_ref = refs[:6]
    pos = 6
    if proj:
        f_ref, wo_ref = refs[pos:pos + 2]
        pos += 2
    if emit_h:
        modn_ref, gn_ref = refs[pos:pos + 2]
        pos += 2
    o_ref = refs[pos]
    pos += 1
    if emit_h:
        hn_ref = refs[pos]
        pos += 1
    h_scr, acc_scr = refs[pos:pos + 2]

    mod = mod_ref[0, 0]
    if proj:
        m = jnp.dot(f_ref[0], wo_ref[0], preferred_element_type=F32)
        o_ref[0] = x_ref[0] + mod[2:3] * m
        res_ref = o_ref
    else:
        res_ref = x_ref
    h_scr[...] = _norm_mod(res_ref[0], g_ref[0], mod[3:4], mod[4:5]).astype(BF16)
    acc_scr[...] = jnp.zeros_like(acc_scr)

    for j in range(n_chunks):
        gu = jnp.dot(h_scr[...], wgu_ref[0, j], preferred_element_type=F32)
        g = gu[:, :chunk]
        u = gu[:, chunk:]
        a = (g * jax.nn.sigmoid(g) * u).astype(BF16)
        acc_scr[...] += jnp.dot(a, wd_ref[0, j], preferred_element_type=F32)
    y = res_ref[0] + mod[5:6] * acc_scr[...]
    if emit_h:
        modn = modn_ref[0, 0]
        hn_ref[0] = _norm_mod(y, gn_ref[0], modn[0:1], modn[1:2]).astype(BF16)
    if final:
        y = _rms(y) * fg_ref[...]
    o_ref[0] = y


def _ffn(x, mod, layer, b_off, w, *, final=False, proj=None, emit_h=False):
    n_b, seq, _ = x.shape
    tm = min(FFN_TOKEN_TILE, seq)
    wgu, wd = w["ffn_wgu"], w["ffn_wd"]
    n_chunks = wgu.shape[1]
    chunk = wd.shape[2]
    tile = pl.BlockSpec((1, tm, D_MODEL), lambda b, i: (b, i, 0))
    args = [x, mod, w["norm2_g"], wgu, wd, w["final_g"]]
    in_specs = [
        tile,
        _mod_spec(layer, b_off),
        pl.BlockSpec((1, 1, D_MODEL), lambda b, i: (layer, 0, 0)),
        _resident((1, n_chunks, D_MODEL, 2 * chunk), lambda b, i: (layer, 0, 0, 0)),
        _resident((1, n_chunks, chunk, D_MODEL), lambda b, i: (layer, 0, 0, 0)),
        pl.BlockSpec((1, D_MODEL), lambda b, i: (0, 0)),
    ]
    if proj is not None:
        f, w_o, j = proj
        args += [f, w_o]
        in_specs += [tile, _resident((1, D_MODEL, D_MODEL), lambda b, i: (j, 0, 0))]
    out_shape = [jax.ShapeDtypeStruct(x.shape, F32)]
    out_specs = [tile]
    if emit_h:
        args += [mod, w["norm1_g"]]
        in_specs += [_mod_spec(layer + 1, b_off),
                     pl.BlockSpec((1, 1, D_MODEL), lambda b, i: (layer + 1, 0, 0))]
        out_shape.append(jax.ShapeDtypeStruct(x.shape, BF16))
        out_specs.append(tile)
    kern = functools.partial(_ffn_kernel, n_chunks=n_chunks, chunk=chunk, final=final,
                             proj=proj is not None, emit_h=emit_h)
    out = pl.pallas_call(
        kern,
        grid=(n_b, seq // tm),
        in_specs=in_specs,
        out_specs=out_specs,
        out_shape=out_shape,
        scratch_shapes=[pltpu.VMEM((tm, D_MODEL), BF16), pltpu.VMEM((tm, D_MODEL), F32)],
        compiler_params=_params(2),
        name="ffn",
    )(*args)
    return out if emit_h else out[0]


def _hnorm_kernel(x_ref, mod_ref, g_ref, o_ref):
    mod = mod_ref[0, 0]
    o_ref[0] = _norm_mod(x_ref[0], g_ref[0], mod[0:1], mod[1:2]).astype(BF16)


def _hnorm(x, mod, layer, b_off, norm_g):
    n_b, seq, _ = x.shape
    tm = min(FFN_TOKEN_TILE, seq)
    tile = pl.BlockSpec((1, tm, D_MODEL), lambda b, i: (b, i, 0))
    return pl.pallas_call(
        _hnorm_kernel,
        grid=(n_b, seq // tm),
        in_specs=[tile, _mod_spec(layer, b_off),
                  pl.BlockSpec((1, 1, D_MODEL), lambda b, i: (layer, 0, 0))],
        out_specs=tile,
        out_shape=jax.ShapeDtypeStruct(x.shape, BF16),
        compiler_params=_params(2),
        name="mixer_input_norm",
    )(x, mod, norm_g)


def _fnet_dft_kernel(h_ref, wc_ref, wa_ref, mb_ref, o_ref, rg_scr, ab_scr, v_scr,
                     *, n2, tc, scale):
    seq = h_ref.shape[1]
    gd = FNET_GROUP_DIM
    k1l = DFT_LONG // n2
    rows = tc // n2
    nblk = gd // LANES
    for c in range(seq // tc):
        ab = jnp.dot(h_ref[0, c * tc:(c + 1) * tc, :], wc_ref[...], preferred_element_type=F32)
        for part in range(2):
            for t in range(nblk):
                lo = part * gd + t * LANES
                rg_scr[part * nblk + t] = ab[:, lo:lo + LANES]
        for j in range(n2):
            for part in range(2):
                for t in range(nblk):
                    sel = rg_scr[part * nblk + t, pl.ds(j, rows, stride=n2), :]
                    ab_scr[part, j, c * rows:(c + 1) * rows, t * LANES:(t + 1) * LANES] = (
                        sel.astype(BF16))
    for j in range(n2):
        ab = ab_scr[:, j].reshape(2 * DFT_LONG, gd)
        v = jnp.dot(wa_ref[j], ab, preferred_element_type=F32)
        v_scr[:, :, j] = v.reshape(2, n2, k1l, gd).astype(BF16)
    for kb in range(n2):
        vb = v_scr[:, kb].reshape(2 * DFT_LONG, gd)
        f = jnp.dot(mb_ref[...], vb, preferred_element_type=F32) * scale
        o_ref[0, :, kb] = f.reshape(n2, k1l, gd).astype(BF16)


def _fnet_tables(seq):
    n2 = seq // DFT_LONG
    k1l = DFT_LONG // n2
    two_pi = 2.0 * math.pi
    gd = FNET_GROUP_DIM
    c = jnp.arange(gd, dtype=jnp.int32)
    ang = ((c[:, None] * c[None, :]) % gd).astype(F32) * (two_pi / gd)
    inv = 1.0 / math.sqrt(gd)
    wc = jnp.concatenate([jnp.cos(ang) * inv, jnp.sin(ang) * inv], axis=1).astype(BF16)

    k1 = jnp.arange(DFT_LONG, dtype=jnp.int32)
    n1 = jnp.arange(DFT_LONG, dtype=jnp.int32)
    j2 = jnp.arange(n2, dtype=jnp.int32)
    sp = n2 * n1[None, None, :] + j2[:, None, None]
    ang = ((k1[None, :, None] * sp) % seq).astype(F32) * (two_pi / seq)
    ca, sa = jnp.cos(ang), jnp.sin(ang)
    wa = jnp.concatenate([jnp.concatenate([ca, -sa], axis=2),
                          jnp.concatenate([sa, ca], axis=2)], axis=1).astype(BF16)

    ang = ((j2[:, None] * j2[None, :]) % n2).astype(F32) * (two_pi / n2)
    eye = jnp.eye(k1l, dtype=F32)
    cb = jnp.einsum("kn,ab->kanb", jnp.cos(ang), eye).reshape(DFT_LONG, DFT_LONG)
    sb = jnp.einsum("kn,ab->kanb", jnp.sin(ang), eye).reshape(DFT_LONG, DFT_LONG)
    mb = jnp.concatenate([cb, -sb], axis=1).astype(BF16)
    return wc, wa, mb


def _fnet_dft(h):
    n_b, seq, _ = h.shape
    n2 = seq // DFT_LONG
    k1l = DFT_LONG // n2
    tc = min(TOKEN_TILE, seq)
    gd = FNET_GROUP_DIM
    wc, wa, mb = _fnet_tables(seq)
    out = pl.pallas_call(
        functools.partial(_fnet_dft_kernel, n2=n2, tc=tc, scale=1.0 / math.sqrt(seq)),
        grid=(n_b, FNET_GROUPS),
        in_specs=[
            pl.BlockSpec((1, seq, gd), lambda b, g: (b, 0, g)),
            pl.BlockSpec((gd, 2 * gd), lambda b, g: (0, 0)),
            _resident((n2, 2 * DFT_LONG, 2 * DFT_LONG), lambda b, g: (0, 0, 0)),
            pl.BlockSpec((DFT_LONG, 2 * DFT_LONG), lambda b, g: (0, 0)),
        ],
        out_specs=pl.BlockSpec((1, n2, n2, k1l, gd), lambda b, g: (b, 0, 0, 0, g)),
        out_shape=jax.ShapeDtypeStruct((n_b, n2, n2, k1l, D_MODEL), BF16),
        scratch_shapes=[
            pltpu.VMEM((2 * gd // LANES, tc, LANES), F32),
            pltpu.VMEM((2, n2, DFT_LONG, gd), BF16),
            pltpu.VMEM((2, n2, n2, k1l, gd), BF16),
        ],
        compiler_params=_params(2),
        name="fnet_dft",
    )(h, wc, wa, mb)
    return out.reshape(n_b, seq, D_MODEL)


def _attn_qkv_kernel(x_ref, mod_ref, g_ref, w_ref, qg_ref, kg_ref, cos_ref, sin_ref,
                     q_ref, k_ref, v_ref):
    x = x_ref[0]
    mod = mod_ref[0, 0]
    h = _norm_mod(x, g_ref[0], mod[0:1], mod[1:2]).astype(BF16)
    qkv = jnp.dot(h, w_ref[0], preferred_element_type=F32)
    cos = cos_ref[...]
    sin = sin_ref[...]
    quarter = HEAD_DIM // 4
    lane = lax.broadcasted_iota(jnp.int32, (x.shape[0], HEAD_DIM), 1)
    first = (lane % (2 * quarter)) < quarter

    def head(t, gain, scale):
        t = _rms(t) * gain
        partner = jnp.where(first, pltpu.roll(t, HEAD_DIM - quarter, 1),
                            pltpu.roll(t, quarter, 1))
        return ((t * cos + partner * sin) * scale).astype(BF16)

    hd = HEAD_DIM
    for i in range(N_HEADS):
        q_ref[0, :, i * hd:(i + 1) * hd] = head(qkv[:, i * hd:(i + 1) * hd], qg_ref[0],
                                                 HEAD_DIM ** -0.5 * LOG2_E)
    k0 = N_HEADS * hd
    v0 = k0 + N_KV_HEADS * hd
    ones = jnp.ones((x.shape[0], hd), BF16)
    for i in range(N_KV_HEADS):
        k_ref[0, :, i * hd:(i + 1) * hd] = head(qkv[:, k0 + i * hd:k0 + (i + 1) * hd],
                                                 kg_ref[0], 1.0)
        v_ref[0, :, 2 * i * hd:(2 * i + 1) * hd] = qkv[:, v0 + i * hd:v0 + (i + 1) * hd].astype(BF16)
        v_ref[0, :, (2 * i + 1) * hd:(2 * i + 2) * hd] = ones


def _attn_kernel(q_ref, k_ref, v_ref, x_ref, mod_ref, wo_ref, o_ref, o_scr):
    hd = HEAD_DIM
    for g in range(N_KV_HEADS):
        k = k_ref[0, :, g * hd:(g + 1) * hd]
        v1 = v_ref[0, :, 2 * g * hd:(2 * g + 2) * hd]
        for i in range(Q_PER_KV):
            hh = g * Q_PER_KV + i
            q = q_ref[0, :, hh * hd:(hh + 1) * hd]
            s = lax.dot_general(q, k, (((1,), (1,)), ((), ())), preferred_element_type=F32)
            p = jnp.exp2(s - jnp.max(s, axis=-1, keepdims=True)).astype(BF16)
            pv = jnp.dot(p, v1, preferred_element_type=F32)
            o = pv[:, :hd] / pv[:, hd:hd + 1]
            o_scr[:, hh * hd:(hh + 1) * hd] = o.astype(BF16)
    m = jnp.dot(o_scr[...], wo_ref[0], preferred_element_type=F32)
    o_ref[0] = x_ref[0] + mod_ref[0, 0][2:3] * m


def _rope_tables(seq):
    pos = jnp.arange(seq, dtype=jnp.int32)
    row = (pos // GRID_W).astype(F32)
    col = (pos % GRID_W).astype(F32)
    half = HEAD_DIM // 2
    freqs = 1.0 / (ROPE_THETA ** (jnp.arange(0, half, 2, dtype=F32) / half))
    ang_r = row[:, None] * freqs[None, :]
    ang_c = col[:, None] * freqs[None, :]
    cos = jnp.concatenate([jnp.cos(ang_r), jnp.cos(ang_r), jnp.cos(ang_c), jnp.cos(ang_c)], axis=1)
    sin = jnp.concatenate([-jnp.sin(ang_r), jnp.sin(ang_r), -jnp.sin(ang_c), jnp.sin(ang_c)], axis=1)
    return cos, sin


def _attention(x, mod, layer, b_off, norm_g, w_qkv, q_g, k_g, w_o, j):
    n_b, seq, _ = x.shape
    tm = min(TOKEN_TILE, seq)
    cos, sin = _rope_tables(seq)
    kv_dim = N_KV_HEADS * HEAD_DIM
    q, k, v = pl.pallas_call(
        _attn_qkv_kernel,
        grid=(n_b, seq // tm),
        in_specs=[
            pl.BlockSpec((1, tm, D_MODEL), lambda b, i: (b, i, 0)),
            _mod_spec(layer, b_off),
            pl.BlockSpec((1, 1, D_MODEL), lambda b, i: (layer, 0, 0)),
            pl.BlockSpec((1, D_MODEL, QKV_DIM), lambda b, i: (j, 0, 0)),
            pl.BlockSpec((1, 1, HEAD_DIM), lambda b, i: (j, 0, 0)),
            pl.BlockSpec((1, 1, HEAD_DIM), lambda b, i: (j, 0, 0)),
            pl.BlockSpec((tm, HEAD_DIM), lambda b, i: (i, 0)),
            pl.BlockSpec((tm, HEAD_DIM), lambda b, i: (i, 0)),
        ],
        out_specs=[
            pl.BlockSpec((1, tm, D_MODEL), lambda b, i: (b, i, 0)),
            pl.BlockSpec((1, tm, kv_dim), lambda b, i: (b, i, 0)),
            pl.BlockSpec((1, tm, 2 * kv_dim), lambda b, i: (b, i, 0)),
        ],
        out_shape=[
            jax.ShapeDtypeStruct((n_b, seq, D_MODEL), BF16),
            jax.ShapeDtypeStruct((n_b, seq, kv_dim), BF16),
            jax.ShapeDtypeStruct((n_b, seq, 2 * kv_dim), BF16),
        ],
        compiler_params=_params(2),
        name="attn_qkv_rope",
    )(x, mod, norm_g, w_qkv, q_g, k_g, cos, sin)

    tq = min(ATTN_Q_TILE, seq)
    return pl.pallas_call(
        _attn_kernel,
        grid=(n_b, seq // tq),
        in_specs=[
            pl.BlockSpec((1, tq, D_MODEL), lambda b, i: (b, i, 0)),
            pl.BlockSpec((1, seq, kv_dim), lambda b, i: (b, 0, 0)),
            pl.BlockSpec((1, seq, 2 * kv_dim), lambda b, i: (b, 0, 0)),
            pl.BlockSpec((1, tq, D_MODEL), lambda b, i: (b, i, 0)),
            _mod_spec(layer, b_off),
            pl.BlockSpec((1, D_MODEL, D_MODEL), lambda b, i: (j, 0, 0)),
        ],
        out_specs=pl.BlockSpec((1, tq, D_MODEL), lambda b, i: (b, i, 0)),
        out_shape=jax.ShapeDtypeStruct(x.shape, F32),
        scratch_shapes=[pltpu.VMEM((tq, D_MODEL), BF16)],
        compiler_params=_params(2),
        name="attn_softmax_out",
    )(q, k, v, x, mod, w_o)


def _sgu_kernel(x_ref, mod_ref, g_ref, win_ref, lng_ref, lnb_ref, ws_ref, bias_ref, wo_ref,
                o_ref, y_scr):
    x = x_ref[0]
    mod = mod_ref[0, 0]
    h = _norm_mod(x, g_ref[0], mod[0:1], mod[1:2]).astype(BF16)
    uv = jnp.dot(h, win_ref[0], preferred_element_type=F32)
    uv = 0.5 * uv * (1.0 + lax.erf(uv * math.sqrt(0.5)))
    u = uv[:, :D_MODEL]
    v = uv[:, D_MODEL:]
    vc = v - jnp.mean(v, axis=-1, keepdims=True)
    vn = vc * lax.rsqrt(jnp.mean(vc * vc, axis=-1, keepdims=True) + EPS)
    vn = (vn * lng_ref[0] + lnb_ref[0]).astype(BF16)
    ck = SGU_CHUNK
    gd = D_MODEL // SGU_GROUPS
    bias = bias_ref[0]
    for c in range(x.shape[0] // ck):
        for g in range(SGU_GROUPS):
            sv = jnp.dot(ws_ref[0, g], vn[c * ck:(c + 1) * ck, g * gd:(g + 1) * gd],
                         preferred_element_type=F32) + bias[:, g * gd:(g + 1) * gd]
            y_scr[c * ck:(c + 1) * ck, g * gd:(g + 1) * gd] = (
                u[c * ck:(c + 1) * ck, g * gd:(g + 1) * gd] * sv).astype(BF16)
    m = jnp.dot(y_scr[...], wo_ref[0], preferred_element_type=F32)
    o_ref[0] = x + mod[2:3] * m


def _sgu(x, mod, layer, b_off, norm_g, w_in, ln_g, ln_b, w_s, bias, w_o, j):
    n_b, seq, _ = x.shape
    tm = min(TOKEN_TILE, seq)
    return pl.pallas_call(
        _sgu_kernel,
        grid=(n_b, seq // tm),
        in_specs=[
            pl.BlockSpec((1, tm, D_MODEL), lambda b, i: (b, i, 0)),
            _mod_spec(layer, b_off),
            pl.BlockSpec((1, 1, D_MODEL), lambda b, i: (layer, 0, 0)),
            pl.BlockSpec((1, D_MODEL, 2 * D_MODEL), lambda b, i: (j, 0, 0)),
            pl.BlockSpec((1, 1, D_MODEL), lambda b, i: (j, 0, 0)),
            pl.BlockSpec((1, 1, D_MODEL), lambda b, i: (j, 0, 0)),
            pl.BlockSpec((1, SGU_GROUPS, SGU_CHUNK, SGU_CHUNK), lambda b, i: (j, 0, 0, 0)),
            pl.BlockSpec((1, SGU_CHUNK, D_MODEL), lambda b, i: (j, 0, 0)),
            pl.BlockSpec((1, D_MODEL, D_MODEL), lambda b, i: (j, 0, 0)),
        ],
        out_specs=pl.BlockSpec((1, tm, D_MODEL), lambda b, i: (b, i, 0)),
        out_shape=jax.ShapeDtypeStruct(x.shape, F32),
        scratch_shapes=[pltpu.VMEM((tm, D_MODEL), BF16)],
        compiler_params=_params(2),
        name="sgu",
    )(x, mod, norm_g, w_in, ln_g, ln_b, w_s, bias, w_o)


def _trunk(x, mod, b_off, w):
    h_next = None
    for i in range(DEPTH):
        kind, j = i % 3, i // 3
        last = i == DEPTH - 1
        emit_h = (not last) and (i + 1) % 3 == 0
        proj = None
        if kind == 0:
            h = h_next if h_next is not None else _hnorm(x, mod, i, b_off, w["norm1_g"])
            proj = (_fnet_dft(h), w["fnet_w_o"], j)
        elif kind == 1:
            x = _attention(x, mod, i, b_off, w["norm1_g"], w["attn_w_qkv"], w["attn_q_g"],
                           w["attn_k_g"], w["attn_w_o"], j)
        else:
            x = _sgu(x, mod, i, b_off, w["norm1_g"], w["sgu_w_in"], w["sgu_ln_g"], w["sgu_ln_b"],
                     w["sgu_w_s"], w["sgu_bias"], w["sgu_w_o"], j)
        out = _ffn(x, mod, i, b_off, w, final=last, proj=proj, emit_h=emit_h)
        x, h_next = out if emit_h else (out, None)
    return x


def kernel(x_prompt, x_sample, c_prompt, c_sample, norm1_g, norm2_g, w_ada, b_ada, fnet_w_o,
           attn_w_qkv, attn_q_g, attn_k_g, attn_w_o, sgu_w_in, sgu_ln_g, sgu_ln_b, sgu_w_s, sgu_b_s,
           sgu_w_o, ffn_w_gu, ffn_w_down, final_g):
    depth = norm1_g.shape[0]
    hidden = ffn_w_down.shape[1]
    n_chunks = hidden // FFN_CHUNK
    wg = ffn_w_gu[:, :, :hidden].reshape(depth, D_MODEL, n_chunks, FFN_CHUNK)
    wu = ffn_w_gu[:, :, hidden:].reshape(depth, D_MODEL, n_chunks, FFN_CHUNK)
    wgu = jnp.concatenate([wg, wu], axis=-1).transpose(0, 2, 1, 3).astype(BF16)
    wd = ffn_w_down.reshape(depth, n_chunks, FFN_CHUNK, D_MODEL).astype(BF16)
    gd = D_MODEL // SGU_GROUPS
    sgu_bias = jnp.repeat(jnp.swapaxes(sgu_b_s, 1, 2), gd, axis=2)
    w = {
        "norm1_g": norm1_g.reshape(depth, 1, D_MODEL),
        "norm2_g": norm2_g.reshape(depth, 1, D_MODEL),
        "fnet_w_o": fnet_w_o.astype(BF16),
        "attn_w_qkv": attn_w_qkv.astype(BF16),
        "attn_q_g": attn_q_g.reshape(-1, 1, HEAD_DIM),
        "attn_k_g": attn_k_g.reshape(-1, 1, HEAD_DIM),
        "attn_w_o": attn_w_o.astype(BF16),
        "sgu_w_in": sgu_w_in.astype(BF16),
        "sgu_ln_g": sgu_ln_g.reshape(-1, 1, D_MODEL),
        "sgu_ln_b": sgu_ln_b.reshape(-1, 1, D_MODEL),
        "sgu_w_s": sgu_w_s.astype(BF16),
        "sgu_bias": sgu_bias,
        "sgu_w_o": sgu_w_o.astype(BF16),
        "ffn_wgu": wgu,
        "ffn_wd": wd,
        "final_g": final_g.reshape(1, D_MODEL),
    }
    c_all = jnp.concatenate([c_prompt, c_sample], axis=0)
    mod = _modulation(c_all, w_ada, b_ada)
    y_prompt = _trunk(x_prompt, mod, 0, w)
    y_sample = _trunk(x_sample, mod, x_prompt.shape[0], w)
    return (y_prompt, y_sample)
```

```python
import functools
import math

import numpy as np

import jax
import jax.numpy as jnp
from jax import lax
from jax.experimental import pallas as pl
from jax.experimental.pallas import tpu as pltpu

F32 = jnp.float32
BF16 = jnp.bfloat16

D_MODEL = 1024
DEPTH = 4
EPS = 1e-6
GRID_W = 64
FNET_GROUPS = 4
FNET_GROUP_DIM = D_MODEL // FNET_GROUPS
HEAD_DIM = 128
N_HEADS = 8
N_KV_HEADS = 2
Q_PER_KV = N_HEADS // N_KV_HEADS
QKV_DIM = (N_HEADS + 2 * N_KV_HEADS) * HEAD_DIM
ROPE_THETA = 10000.0
SGU_GROUPS = 8
SGU_CHUNK = 128
FFN_HIDDEN = 2816

DFT_LONG = 256
LANES = 128
SUBLANES = 8
TOKEN_TILE = 512
FFN_TOKEN_TILE = 1024
LOG2_E = 1.4426950408889634
ATTN_Q_TILE = 512
SUB_TILE = 256
SGU_SUB_TILE = 512
FFN_CHUNK = 256
VMEM_LIMIT = 56 * 1024 * 1024


def _params(n_axes):
    return pltpu.CompilerParams(
        dimension_semantics=("parallel",) * n_axes, vmem_limit_bytes=VMEM_LIMIT)


def _resident(block_shape, index_map):
    return pl.BlockSpec(block_shape, index_map, pipeline_mode=pl.Buffered(1))


def _rms(x):
    return x * lax.rsqrt(jnp.mean(x * x, axis=-1, keepdims=True) + EPS)


def _norm_mod(x, gain, shift, scale):
    return _rms(x) * (gain * (1.0 + scale)) + shift


def _mod_kernel(c_ref, w_ref, b_ref, o_ref):
    c = c_ref[...]
    cs = c * jax.nn.sigmoid(c)
    o_ref[0] = jnp.dot(cs, w_ref[0], precision=lax.Precision.HIGHEST,
                       preferred_element_type=F32) + b_ref[0]


def _modulation(c_all, w_ada, b_ada):
    n_b = c_all.shape[0]
    tn = 1536
    out = pl.pallas_call(
        _mod_kernel,
        grid=(DEPTH, 6 * D_MODEL // tn),
        in_specs=[
            pl.BlockSpec((n_b, D_MODEL), lambda l, j: (0, 0)),
            pl.BlockSpec((1, D_MODEL, tn), lambda l, j: (l, 0, j)),
            pl.BlockSpec((1, 1, tn), lambda l, j: (l, 0, j)),
        ],
        out_specs=pl.BlockSpec((1, n_b, tn), lambda l, j: (l, 0, j)),
        out_shape=jax.ShapeDtypeStruct((DEPTH, n_b, 6 * D_MODEL), F32),
        compiler_params=_params(2),
        name="adaln_mod",
    )(c_all, w_ada, b_ada.reshape(DEPTH, 1, 6 * D_MODEL))
    return out.reshape(DEPTH, n_b, 6, D_MODEL)


def _mod_spec(layer, b_off):
    return pl.BlockSpec((1, 1, 6, D_MODEL), lambda b, i: (layer, b_off + b, 0, 0))


def _grouped_shape(n_b, seq):
    return jax.ShapeDtypeStruct((n_b, FNET_GROUPS, seq, FNET_GROUP_DIM), BF16)


def _grouped_tile(tm):
    return pl.BlockSpec((1, FNET_GROUPS, tm, FNET_GROUP_DIM), lambda b, i: (b, 0, i, 0))


def _store_grouped(ref, value):
    for g in range(FNET_GROUPS):
        ref[0, g] = value[:, g * FNET_GROUP_DIM:(g + 1) * FNET_GROUP_DIM]


def _ffn_kernel(*refs, n_chunks, chunk, final, proj, emit_h):
    refs = list(refs)
    x_ref, mod_ref, g_ref, wgu_ref, wd_ref, fg_ref = refs[:6]
    pos = 6
    if proj:
        f_ref, wo_ref = refs[pos:pos + 2]
        pos += 2
    if emit_h:
        modn_ref, gn_ref = refs[pos:pos + 2]
        pos += 2
    o_ref = refs[pos]
    pos += 1
    if emit_h:
        hn_ref = refs[pos]
        pos += 1
    h_scr, acc_scr = refs[pos:pos + 2]

    mod = mod_ref[0, 0]
    if proj:
        f = jnp.concatenate([f_ref[0, g] for g in range(FNET_GROUPS)], axis=-1)
        m = jnp.dot(f, wo_ref[0], preferred_element_type=F32)
        o_ref[0] = x_ref[0] + mod[2:3] * m
        res_ref = o_ref
    else:
        res_ref = x_ref
    h_scr[...] = _norm_mod(res_ref[0], g_ref[0], mod[3:4], mod[4:5]).astype(BF16)
    acc_scr[...] = jnp.zeros_like(acc_scr)

    hidden = n_chunks * chunk
    for j in range(n_chunks):
        lo = j * chunk
        g = jnp.dot(h_scr[...], wgu_ref[0, :, lo:lo + chunk], preferred_element_type=F32)
        u = jnp.dot(h_scr[...], wgu_ref[0, :, hidden + lo:hidden + lo + chunk],
                    preferred_element_type=F32)
        a = (g * jax.nn.sigmoid(g) * u).astype(BF16)
        acc_scr[...] += jnp.dot(a, wd_ref[0, lo:lo + chunk, :], preferred_element_type=F32)
    y = res_ref[0] + mod[5:6] * acc_scr[...]
    if emit_h:
        modn = modn_ref[0, 0]
        _store_grouped(hn_ref, _norm_mod(y, gn_ref[0], modn[0:1], modn[1:2]).astype(BF16))
    if final:
        y = _rms(y) * fg_ref[...]
    o_ref[0] = y


def _ffn(x, mod, layer, b_off, w, *, final=False, proj=None, emit_h=False):
    n_b, seq, _ = x.shape
    tm = min(FFN_TOKEN_TILE, seq)
    wgu, wd = w["ffn_wgu"], w["ffn_wd"]
    hidden = wd.shape[1]
    chunk = FFN_CHUNK
    n_chunks = hidden // chunk
    tile = pl.BlockSpec((1, tm, D_MODEL), lambda b, i: (b, i, 0))
    args = [x, mod, w["norm2_g"], wgu, wd, w["final_g"]]
    in_specs = [
        tile,
        _mod_spec(layer, b_off),
        pl.BlockSpec((1, 1, D_MODEL), lambda b, i: (layer, 0, 0)),
        _resident((1, D_MODEL, 2 * hidden), lambda b, i: (layer, 0, 0)),
        _resident((1, hidden, D_MODEL), lambda b, i: (layer, 0, 0)),
        pl.BlockSpec((1, D_MODEL), lambda b, i: (0, 0)),
    ]
    if proj is not None:
        f, w_o, j = proj
        args += [f, w_o]
        in_specs += [_grouped_tile(tm), _resident((1, D_MODEL, D_MODEL), lambda b, i: (j, 0, 0))]
    out_shape = [jax.ShapeDtypeStruct(x.shape, F32)]
    out_specs = [tile]
    if emit_h:
        args += [mod, w["norm1_g"]]
        in_specs += [_mod_spec(layer + 1, b_off),
                     pl.BlockSpec((1, 1, D_MODEL), lambda b, i: (layer + 1, 0, 0))]
        out_shape.append(_grouped_shape(n_b, seq))
        out_specs.append(_grouped_tile(tm))
    kern = functools.partial(_ffn_kernel, n_chunks=n_chunks, chunk=chunk, final=final,
                             proj=proj is not None, emit_h=emit_h)
    out = pl.pallas_call(
        kern,
        grid=(n_b, seq // tm),
        in_specs=in_specs,
        out_specs=out_specs,
        out_shape=out_shape,
        scratch_shapes=[pltpu.VMEM((tm, D_MODEL), BF16), pltpu.VMEM((tm, D_MODEL), F32)],
        compiler_params=_params(2),
        name="ffn",
    )(*args)
    return out if emit_h else out[0]


def _hnorm_kernel(x_ref, mod_ref, g_ref, o_ref):
    mod = mod_ref[0, 0]
    _store_grouped(o_ref, _norm_mod(x_ref[0], g_ref[0], mod[0:1], mod[1:2]).astype(BF16))


def _hnorm(x, mod, layer, b_off, norm_g):
    n_b, seq, _ = x.shape
    tm = min(FFN_TOKEN_TILE, seq)
    tile = pl.BlockSpec((1, tm, D_MODEL), lambda b, i: (b, i, 0))
    return pl.pallas_call(
        _hnorm_kernel,
        grid=(n_b, seq // tm),
        in_specs=[tile, _mod_spec(layer, b_off),
                  pl.BlockSpec((1, 1, D_MODEL), lambda b, i: (layer, 0, 0))],
        out_specs=_grouped_tile(tm),
        out_shape=_grouped_shape(n_b, seq),
        compiler_params=_params(2),
        name="mixer_input_norm",
    )(x, mod, norm_g)


def _fnet_dft_kernel(h_ref, wc_ref, wa_ref, mb_ref, o_ref, rg_scr, ab_scr, v_scr,
                     *, n2, tc, pitch, scale):
    seq = h_ref.shape[2]
    gd = FNET_GROUP_DIM
    k1l = DFT_LONG // n2
    nblk = gd // LANES
    for c in range(seq // tc):
        ab = jnp.dot(h_ref[0, 0, c * tc:(c + 1) * tc, :], wc_ref[...], preferred_element_type=F32)
        for part in range(2):
            for t in range(nblk):
                lo = part * gd + t * LANES
                if pitch == n2:
                    rg_scr[part * nblk + t, c * tc:(c + 1) * tc, :] = ab[:, lo:lo + LANES]
                else:
                    for q in range(tc // n2):
                        r0 = (c * (tc // n2) + q) * pitch
                        rg_scr[part * nblk + t, r0:r0 + n2, :] = ab[q * n2:(q + 1) * n2, lo:lo + LANES]
    for j in range(n2):
        for part in range(2):
            for t in range(nblk):
                sel = rg_scr[part * nblk + t, pl.ds(j, DFT_LONG, stride=pitch), :]
                ab_scr[part, j, :, t * LANES:(t + 1) * LANES] = sel.astype(BF16)
    for j in range(n2):
        ab = ab_scr[:, j].reshape(2 * DFT_LONG, gd)
        v = jnp.dot(wa_ref[j], ab, preferred_element_type=F32)
        v_scr[:, :, j] = v.reshape(2, n2, k1l, gd).astype(BF16)
    for kb in range(n2):
        vb = v_scr[:, kb].reshape(2 * DFT_LONG, gd)
        f = jnp.dot(mb_ref[...], vb, preferred_element_type=F32) * scale
        o_ref[0, 0, :, kb] = f.reshape(n2, k1l, gd).astype(BF16)


def _fnet_tables(seq):
    n2 = seq // DFT_LONG
    k1l = DFT_LONG // n2
    two_pi = 2.0 * math.pi
    gd = FNET_GROUP_DIM
    c = jnp.arange(gd, dtype=jnp.int32)
    ang = ((c[:, None] * c[None, :]) % gd).astype(F32) * (two_pi / gd)
    inv = 1.0 / math.sqrt(gd)
    wc = jnp.concatenate([jnp.cos(ang) * inv, jnp.sin(ang) * inv], axis=1).astype(BF16)

    k1 = jnp.arange(DFT_LONG, dtype=jnp.int32)
    n1 = jnp.arange(DFT_LONG, dtype=jnp.int32)
    j2 = jnp.arange(n2, dtype=jnp.int32)
    sp = n2 * n1[None, None, :] + j2[:, None, None]
    ang = ((k1[None, :, None] * sp) % seq).astype(F32) * (two_pi / seq)
    ca, sa = jnp.cos(ang), jnp.sin(ang)
    wa = jnp.concatenate([jnp.concatenate([ca, -sa], axis=2),
                          jnp.concatenate([sa, ca], axis=2)], axis=1).astype(BF16)

    ang = ((j2[:, None] * j2[None, :]) % n2).astype(F32) * (two_pi / n2)
    eye = jnp.eye(k1l, dtype=F32)
    cb = jnp.einsum("kn,ab->kanb", jnp.cos(ang), eye).reshape(DFT_LONG, DFT_LONG)
    sb = jnp.einsum("kn,ab->kanb", jnp.sin(ang), eye).reshape(DFT_LONG, DFT_LONG)
    mb = jnp.concatenate([cb, -sb], axis=1).astype(BF16)
    return wc, wa, mb


def _fnet_dft(h):
    n_b, _, seq, _ = h.shape
    n2 = seq // DFT_LONG
    k1l = DFT_LONG // n2
    tc = min(TOKEN_TILE, seq)
    gd = FNET_GROUP_DIM
    wc, wa, mb = _fnet_tables(seq)
    pitch = n2 + SUBLANES // 2 if n2 % SUBLANES == 0 else n2
    out = pl.pallas_call(
        functools.partial(_fnet_dft_kernel, n2=n2, tc=tc, pitch=pitch, scale=1.0 / math.sqrt(seq)),
        grid=(n_b, FNET_GROUPS),
        in_specs=[
            pl.BlockSpec((1, 1, seq, gd), lambda b, g: (b, g, 0, 0)),
            pl.BlockSpec((gd, 2 * gd), lambda b, g: (0, 0)),
            _resident((n2, 2 * DFT_LONG, 2 * DFT_LONG), lambda b, g: (0, 0, 0)),
            pl.BlockSpec((DFT_LONG, 2 * DFT_LONG), lambda b, g: (0, 0)),
        ],
        out_specs=pl.BlockSpec((1, 1, n2, n2, k1l, gd), lambda b, g: (b, g, 0, 0, 0, 0)),
        out_shape=jax.ShapeDtypeStruct((n_b, FNET_GROUPS, n2, n2, k1l, gd), BF16),
        scratch_shapes=[
            pltpu.VMEM((2 * gd // LANES, DFT_LONG * pitch, LANES), F32),
            pltpu.VMEM((2, n2, DFT_LONG, gd), BF16),
            pltpu.VMEM((2, n2, n2, k1l, gd), BF16),
        ],
        compiler_params=_params(2),
        name="fnet_dft",
    )(h, wc, wa, mb)
    return out.reshape(n_b, FNET_GROUPS, seq, gd)


def _attn_qkv_kernel(x_ref, mod_ref, g_ref, w_ref, qg_ref, kg_ref, cos_ref, sin_ref,
                     q_ref, k_ref, v_ref):
    mod = mod_ref[0, 0]
    hd = HEAD_DIM
    k0 = N_HEADS * hd
    v0 = k0 + N_KV_HEADS * hd
    sub = SUB_TILE
    for st in range(x_ref.shape[1] // sub):
        r = slice(st * sub, (st + 1) * sub)
        h = _norm_mod(x_ref[0, r, :], g_ref[0], mod[0:1], mod[1:2]).astype(BF16)
        qkv = jnp.dot(h, w_ref[0], preferred_element_type=F32)
        cos = cos_ref[r, :]
        sin = sin_ref[r, :]

        def head(t, gain, scale):
            t = _rms(t) * gain
            return ((t * cos + pltpu.roll(t, hd // 2, 1) * sin) * scale).astype(BF16)

        for i in range(N_HEADS):
            q_ref[0, r, i * hd:(i + 1) * hd] = head(qkv[:, i * hd:(i + 1) * hd], qg_ref[0],
                                                     HEAD_DIM ** -0.5 * LOG2_E)
        ones = jnp.ones((sub, hd), BF16)
        for i in range(N_KV_HEADS):
            k_ref[0, r, i * hd:(i + 1) * hd] = head(qkv[:, k0 + i * hd:k0 + (i + 1) * hd],
                                                     kg_ref[0], 1.0)
            v_ref[0, r, 2 * i * hd:(2 * i + 1) * hd] = (
                qkv[:, v0 + i * hd:v0 + (i + 1) * hd].astype(BF16))
            v_ref[0, r, (2 * i + 1) * hd:(2 * i + 2) * hd] = ones


def _attn_kernel(q_ref, k_ref, v_ref, x_ref, mod_ref, wo_ref, o_ref, o_scr):
    hd = HEAD_DIM
    for g in range(N_KV_HEADS):
        k = k_ref[0, :, g * hd:(g + 1) * hd]
        v1 = v_ref[0, :, 2 * g * hd:(2 * g + 2) * hd]
        for i in range(Q_PER_KV):
            hh = g * Q_PER_KV + i
            q = q_ref[0, :, hh * hd:(hh + 1) * hd]
            s = lax.dot_general(q, k, (((1,), (1,)), ((), ())), preferred_element_type=F32)
            p = jnp.exp2(s - jnp.max(s, axis=-1, keepdims=True)).astype(BF16)
            pv = jnp.dot(p, v1, preferred_element_type=F32)
            o = pv[:, :hd] / pv[:, hd:hd + 1]
            o_scr[:, hh * hd:(hh + 1) * hd] = o.astype(BF16)
    m = jnp.dot(o_scr[...], wo_ref[0], preferred_element_type=F32)
    o_ref[0] = x_ref[0] + mod_ref[0, 0][2:3] * m


def _rope_lane_order():
    q = HEAD_DIM // 4
    return [*range(0, q), *range(2 * q, 3 * q), *range(q, 2 * q), *range(3 * q, 4 * q)]


def _rope_tables(seq):
    pos = jnp.arange(seq, dtype=jnp.int32)
    row = (pos // GRID_W).astype(F32)
    col = (pos % GRID_W).astype(F32)
    half = HEAD_DIM // 2
    freqs = 1.0 / (ROPE_THETA ** (jnp.arange(0, half, 2, dtype=F32) / half))
    ang_r = row[:, None] * freqs[None, :]
    ang_c = col[:, None] * freqs[None, :]
    cos = jnp.concatenate([jnp.cos(ang_r), jnp.cos(ang_c), jnp.cos(ang_r), jnp.cos(ang_c)], axis=1)
    sin = jnp.concatenate([-jnp.sin(ang_r), -jnp.sin(ang_c), jnp.sin(ang_r), jnp.sin(ang_c)], axis=1)
    return cos, sin


def _attention(x, mod, layer, b_off, norm_g, w_qkv, q_g, k_g, w_o, j):
    n_b, seq, _ = x.shape
    tm = min(TOKEN_TILE, seq)
    cos, sin = _rope_tables(seq)
    kv_dim = N_KV_HEADS * HEAD_DIM
    q, k, v = pl.pallas_call(
        _attn_qkv_kernel,
        grid=(n_b, seq // tm),
        in_specs=[
            pl.BlockSpec((1, tm, D_MODEL), lambda b, i: (b, i, 0)),
            _mod_spec(layer, b_off),
            pl.BlockSpec((1, 1, D_MODEL), lambda b, i: (layer, 0, 0)),
            pl.BlockSpec((1, D_MODEL, QKV_DIM), lambda b, i: (j, 0, 0)),
            pl.BlockSpec((1, 1, HEAD_DIM), lambda b, i: (j, 0, 0)),
            pl.BlockSpec((1, 1, HEAD_DIM), lambda b, i: (j, 0, 0)),
            pl.BlockSpec((tm, HEAD_DIM), lambda b, i: (i, 0)),
            pl.BlockSpec((tm, HEAD_DIM), lambda b, i: (i, 0)),
        ],
        out_specs=[
            pl.BlockSpec((1, tm, D_MODEL), lambda b, i: (b, i, 0)),
            pl.BlockSpec((1, tm, kv_dim), lambda b, i: (b, i, 0)),
            pl.BlockSpec((1, tm, 2 * kv_dim), lambda b, i: (b, i, 0)),
        ],
        out_shape=[
            jax.ShapeDtypeStruct((n_b, seq, D_MODEL), BF16),
            jax.ShapeDtypeStruct((n_b, seq, kv_dim), BF16),
            jax.ShapeDtypeStruct((n_b, seq, 2 * kv_dim), BF16),
        ],
        compiler_params=_params(2),
        name="attn_qkv_rope",
    )(x, mod, norm_g, w_qkv, q_g, k_g, cos, sin)

    tq = min(ATTN_Q_TILE, seq)
    return pl.pallas_call(
        _attn_kernel,
        grid=(n_b, seq // tq),
        in_specs=[
            pl.BlockSpec((1, tq, D_MODEL), lambda b, i: (b, i, 0)),
            pl.BlockSpec((1, seq, kv_dim), lambda b, i: (b, 0, 0)),
            pl.BlockSpec((1, seq, 2 * kv_dim), lambda b, i: (b, 0, 0)),
            pl.BlockSpec((1, tq, D_MODEL), lambda b, i: (b, i, 0)),
            _mod_spec(layer, b_off),
            pl.BlockSpec((1, D_MODEL, D_MODEL), lambda b, i: (j, 0, 0)),
        ],
        out_specs=pl.BlockSpec((1, tq, D_MODEL), lambda b, i: (b, i, 0)),
        out_shape=jax.ShapeDtypeStruct(x.shape, F32),
        scratch_shapes=[pltpu.VMEM((tq, D_MODEL), BF16)],
        compiler_params=_params(2),
        name="attn_softmax_out",
    )(q, k, v, x, mod, w_o)


def _sgu_kernel(x_ref, mod_ref, g_ref, win_ref, lng_ref, lnb_ref, ws_ref, bias_ref, wo_ref,
                o_ref, y_scr):
    mod = mod_ref[0, 0]
    ck = SGU_CHUNK
    gd = D_MODEL // SGU_GROUPS
    bias = bias_ref[0]
    sub = SGU_SUB_TILE
    n_sub = x_ref.shape[1] // sub

    def gate_inputs(st):
        x = x_ref[0, st * sub:(st + 1) * sub, :]
        h = _norm_mod(x, g_ref[0], mod[0:1], mod[1:2]).astype(BF16)
        uv = jnp.dot(h, win_ref[0], preferred_element_type=F32)
        uv = 0.5 * uv * (1.0 + lax.erf(uv * math.sqrt(0.5)))
        u = uv[:, :D_MODEL]
        v = uv[:, D_MODEL:]
        vc = v - jnp.mean(v, axis=-1, keepdims=True)
        vn = vc * lax.rsqrt(jnp.mean(vc * vc, axis=-1, keepdims=True) + EPS)
        return u, (vn * lng_ref[0] + lnb_ref[0]).astype(BF16)

    def gate_and_project(st, u, vn):
        r0 = st * sub
        for c in range(sub // ck):
            for g in range(SGU_GROUPS):
                sv = jnp.dot(ws_ref[0, g], vn[c * ck:(c + 1) * ck, g * gd:(g + 1) * gd],
                             preferred_element_type=F32) + bias[:, g * gd:(g + 1) * gd]
                y_scr[r0 + c * ck:r0 + (c + 1) * ck, g * gd:(g + 1) * gd] = (
                    u[c * ck:(c + 1) * ck, g * gd:(g + 1) * gd] * sv).astype(BF16)
        m = jnp.dot(y_scr[r0:r0 + sub, :], wo_ref[0], preferred_element_type=F32)
        o_ref[0, r0:r0 + sub, :] = x_ref[0, r0:r0 + sub, :] + mod[2:3] * m

    pending = gate_inputs(0)
    for st in range(1, n_sub):
        nxt = gate_inputs(st)
        gate_and_project(st - 1, *pending)
        pending = nxt
    gate_and_project(n_sub - 1, *pending)


def _sgu(x, mod, layer, b_off, norm_g, w_in, ln_g, ln_b, w_s, bias, w_o, j):
    n_b, seq, _ = x.shape
    tm = min(2 * SGU_SUB_TILE, seq)
    return pl.pallas_call(
        _sgu_kernel,
        grid=(n_b, seq // tm),
        in_specs=[
            pl.BlockSpec((1, tm, D_MODEL), lambda b, i: (b, i, 0)),
            _mod_spec(layer, b_off),
            pl.BlockSpec((1, 1, D_MODEL), lambda b, i: (layer, 0, 0)),
            pl.BlockSpec((1, D_MODEL, 2 * D_MODEL), lambda b, i: (j, 0, 0)),
            pl.BlockSpec((1, 1, D_MODEL), lambda b, i: (j, 0, 0)),
            pl.BlockSpec((1, 1, D_MODEL), lambda b, i: (j, 0, 0)),
            pl.BlockSpec((1, SGU_GROUPS, SGU_CHUNK, SGU_CHUNK), lambda b, i: (j, 0, 0, 0)),
            pl.BlockSpec((1, SGU_CHUNK, D_MODEL), lambda b, i: (j, 0, 0)),
            pl.BlockSpec((1, D_MODEL, D_MODEL), lambda b, i: (j, 0, 0)),
        ],
        out_specs=pl.BlockSpec((1, tm, D_MODEL), lambda b, i: (b, i, 0)),
        out_shape=jax.ShapeDtypeStruct(x.shape, F32),
        scratch_shapes=[pltpu.VMEM((tm, D_MODEL), BF16)],
        compiler_params=_params(2),
        name="sgu",
    )(x, mod, norm_g, w_in, ln_g, ln_b, w_s, bias, w_o)


def _trunk(x, mod, b_off, w):
    h_next = None
    for i in range(DEPTH):
        kind, j = i % 3, i // 3
        last = i == DEPTH - 1
        emit_h = (not last) and (i + 1) % 3 == 0
        proj = None
        if kind == 0:
            h = h_next if h_next is not None else _hnorm(x, mod, i, b_off, w["norm1_g"])
            proj = (_fnet_dft(h), w["fnet_w_o"], j)
        elif kind == 1:
            x = _attention(x, mod, i, b_off, w["norm1_g"], w["attn_w_qkv"], w["attn_q_g"],
                           w["attn_k_g"], w["attn_w_o"], j)
        else:
            x = _sgu(x, mod, i, b_off, w["norm1_g"], w["sgu_w_in"], w["sgu_ln_g"], w["sgu_ln_b"],
                     w["sgu_w_s"], w["sgu_bias"], w["sgu_w_o"], j)
        out = _ffn(x, mod, i, b_off, w, final=last, proj=proj, emit_h=emit_h)
        x, h_next = out if emit_h else (out, None)
    return x


def kernel(x_prompt, x_sample, c_prompt, c_sample, norm1_g, norm2_g, w_ada, b_ada, fnet_w_o,
           attn_w_qkv, attn_q_g, attn_k_g, attn_w_o, sgu_w_in, sgu_ln_g, sgu_ln_b, sgu_w_s, sgu_b_s,
           sgu_w_o, ffn_w_gu, ffn_w_down, final_g):
    depth = norm1_g.shape[0]
    gd = D_MODEL // SGU_GROUPS
    lane_order = np.asarray(_rope_lane_order())
    n_rot = (N_HEADS + N_KV_HEADS) * HEAD_DIM
    qkv_cols = np.concatenate([(np.arange(n_rot) // HEAD_DIM) * HEAD_DIM + np.tile(lane_order, n_rot // HEAD_DIM),
                               np.arange(n_rot, QKV_DIM)])
    sgu_bias = jnp.repeat(jnp.swapaxes(sgu_b_s, 1, 2), gd, axis=2)
    w = {
        "norm1_g": norm1_g.reshape(depth, 1, D_MODEL),
        "norm2_g": norm2_g.reshape(depth, 1, D_MODEL),
        "fnet_w_o": fnet_w_o.astype(BF16),
        "attn_w_qkv": attn_w_qkv[:, :, qkv_cols].astype(BF16),
        "attn_q_g": attn_q_g[:, lane_order].reshape(-1, 1, HEAD_DIM),
        "attn_k_g": attn_k_g[:, lane_order].reshape(-1, 1, HEAD_DIM),
        "attn_w_o": attn_w_o.astype(BF16),
        "sgu_w_in": sgu_w_in.astype(BF16),
        "sgu_ln_g": sgu_ln_g.reshape(-1, 1, D_MODEL),
        "sgu_ln_b": sgu_ln_b.reshape(-1, 1, D_MODEL),
        "sgu_w_s": sgu_w_s.astype(BF16),
        "sgu_bias": sgu_bias,
        "sgu_w_o": sgu_w_o.astype(BF16),
        "ffn_wgu": ffn_w_gu.astype(BF16),
        "ffn_wd": ffn_w_down.astype(BF16),
        "final_g": final_g.reshape(1, D_MODEL),
    }
    c_all = jnp.concatenate([c_prompt, c_sample], axis=0)
    mod = _modulation(c_all, w_ada, b_ada)
    y_prompt = _trunk(x_prompt, mod, 0, w)
    y_sample = _trunk(x_sample, mod, x_prompt.shape[0], w)
    return (y_prompt, y_sample)
```

```python
import functools
import math

import numpy as np

import jax
import jax.numpy as jnp
from jax import lax
from jax.experimental import pallas as pl
from jax.experimental.pallas import tpu as pltpu

F32 = jnp.float32
BF16 = jnp.bfloat16

D_MODEL = 1024
DEPTH = 4
EPS = 1e-6
GRID_W = 64
FNET_GROUPS = 4
FNET_GROUP_DIM = D_MODEL // FNET_GROUPS
HEAD_DIM = 128
N_HEADS = 8
N_KV_HEADS = 2
Q_PER_KV = N_HEADS // N_KV_HEADS
QKV_DIM = (N_HEADS + 2 * N_KV_HEADS) * HEAD_DIM
ROPE_THETA = 10000.0
SGU_GROUPS = 8
SGU_CHUNK = 128
FFN_HIDDEN = 2816

DFT_LONG = 256
LANES = 128
SUBLANES = 8
TOKEN_TILE = 512
FFN_TOKEN_TILE = 1024
LOG2_E = 1.4426950408889634
ATTN_Q_TILE = 512
SUB_TILE = 256
SGU_SUB_TILE = 512
DFT_CHANNEL_TILE = 2048
FFN_CHUNK = 256
VMEM_LIMIT = 56 * 1024 * 1024


def _params(n_axes):
    return pltpu.CompilerParams(
        dimension_semantics=("parallel",) * n_axes, vmem_limit_bytes=VMEM_LIMIT)


def _resident(block_shape, index_map):
    return pl.BlockSpec(block_shape, index_map, pipeline_mode=pl.Buffered(1))


def _rms(x):
    return x * lax.rsqrt(jnp.mean(x * x, axis=-1, keepdims=True) + EPS)


def _norm_mod(x, gain, shift, scale):
    return _rms(x) * (gain * (1.0 + scale)) + shift


def _mod_kernel(c_ref, w_ref, b_ref, o_ref):
    c = c_ref[...]
    cs = c * jax.nn.sigmoid(c)
    o_ref[0] = jnp.dot(cs, w_ref[0], precision=lax.Precision.HIGHEST,
                       preferred_element_type=F32) + b_ref[0]


def _modulation(c_all, w_ada, b_ada):
    n_b = c_all.shape[0]
    tn = 1536
    out = pl.pallas_call(
        _mod_kernel,
        grid=(DEPTH, 6 * D_MODEL // tn),
        in_specs=[
            pl.BlockSpec((n_b, D_MODEL), lambda l, j: (0, 0)),
            pl.BlockSpec((1, D_MODEL, tn), lambda l, j: (l, 0, j)),
            pl.BlockSpec((1, 1, tn), lambda l, j: (l, 0, j)),
        ],
        out_specs=pl.BlockSpec((1, n_b, tn), lambda l, j: (l, 0, j)),
        out_shape=jax.ShapeDtypeStruct((DEPTH, n_b, 6 * D_MODEL), F32),
        compiler_params=_params(2),
        name="adaln_mod",
    )(c_all, w_ada, b_ada.reshape(DEPTH, 1, 6 * D_MODEL))
    return out.reshape(DEPTH, n_b, 6, D_MODEL)


def _mod_spec(layer, b_off):
    return pl.BlockSpec((1, 1, 6, D_MODEL), lambda b, i: (layer, b_off + b, 0, 0))


def _grouped_shape(n_b, seq):
    return jax.ShapeDtypeStruct((n_b, FNET_GROUPS, seq, FNET_GROUP_DIM), BF16)


def _grouped_tile(tm):
    return pl.BlockSpec((1, FNET_GROUPS, tm, FNET_GROUP_DIM), lambda b, i: (b, 0, i, 0))


def _store_grouped(ref, value):
    for g in range(FNET_GROUPS):
        ref[0, g] = value[:, g * FNET_GROUP_DIM:(g + 1) * FNET_GROUP_DIM]


def _ffn_kernel(*refs, n_chunks, chunk, final, proj, emit_h):
    refs = list(refs)
    x_ref, mod_ref, g_ref, wgu_ref, wd_ref, fg_ref = refs[:6]
    pos = 6
    if proj:
        f_ref, wo_ref = refs[pos:pos + 2]
        pos += 2
    if emit_h:
        modn_ref, gn_ref = refs[pos:pos + 2]
        pos += 2
    o_ref = refs[pos]
    pos += 1
    if emit_h:
        hn_ref = refs[pos]
        pos += 1
    h_scr, acc_scr = refs[pos:pos + 2]

    mod = mod_ref[0, 0]
    if proj:
        f = jnp.concatenate([f_ref[0, g] for g in range(FNET_GROUPS)], axis=-1)
        m = jnp.dot(f, wo_ref[0], preferred_element_type=F32)
        o_ref[0] = x_ref[0] + mod[2:3] * m
        res_ref = o_ref
    else:
        res_ref = x_ref
    h_scr[...] = _norm_mod(res_ref[0], g_ref[0], mod[3:4], mod[4:5]).astype(BF16)
    acc_scr[...] = jnp.zeros_like(acc_scr)

    hidden = n_chunks * chunk
    for j in range(n_chunks):
        lo = j * chunk
        g = jnp.dot(h_scr[...], wgu_ref[0, :, lo:lo + chunk], preferred_element_type=F32)
        u = jnp.dot(h_scr[...], wgu_ref[0, :, hidden + lo:hidden + lo + chunk],
                    preferred_element_type=F32)
        a = (g * jax.nn.sigmoid(g) * u).astype(BF16)
        acc_scr[...] += jnp.dot(a, wd_ref[0, lo:lo + chunk, :], preferred_element_type=F32)
    y = res_ref[0] + mod[5:6] * acc_scr[...]
    if emit_h:
        modn = modn_ref[0, 0]
        _store_grouped(hn_ref, _norm_mod(y, gn_ref[0], modn[0:1], modn[1:2]).astype(BF16))
    if final:
        y = _rms(y) * fg_ref[...]
    o_ref[0] = y


def _ffn(x, mod, layer, b_off, w, *, final=False, proj=None, emit_h=False):
    n_b, seq, _ = x.shape
    tm = min(FFN_TOKEN_TILE, seq)
    wgu, wd = w["ffn_wgu"], w["ffn_wd"]
    hidden = wd.shape[1]
    chunk = FFN_CHUNK
    n_chunks = hidden // chunk
    tile = pl.BlockSpec((1, tm, D_MODEL), lambda b, i: (b, i, 0))
    args = [x, mod, w["norm2_g"], wgu, wd, w["final_g"]]
    in_specs = [
        tile,
        _mod_spec(layer, b_off),
        pl.BlockSpec((1, 1, D_MODEL), lambda b, i: (layer, 0, 0)),
        _resident((1, D_MODEL, 2 * hidden), lambda b, i: (layer, 0, 0)),
        _resident((1, hidden, D_MODEL), lambda b, i: (layer, 0, 0)),
        pl.BlockSpec((1, D_MODEL), lambda b, i: (0, 0)),
    ]
    if proj is not None:
        f, w_o, j = proj
        args += [f, w_o]
        in_specs += [_grouped_tile(tm), _resident((1, D_MODEL, D_MODEL), lambda b, i: (j, 0, 0))]
    out_shape = [jax.ShapeDtypeStruct(x.shape, F32)]
    out_specs = [tile]
    if emit_h:
        args += [mod, w["norm1_g"]]
        in_specs += [_mod_spec(layer + 1, b_off),
                     pl.BlockSpec((1, 1, D_MODEL), lambda b, i: (layer + 1, 0, 0))]
        out_shape.append(_grouped_shape(n_b, seq))
        out_specs.append(_grouped_tile(tm))
    kern = functools.partial(_ffn_kernel, n_chunks=n_chunks, chunk=chunk, final=final,
                             proj=proj is not None, emit_h=emit_h)
    out = pl.pallas_call(
        kern,
        grid=(n_b, seq // tm),
        in_specs=in_specs,
        out_specs=out_specs,
        out_shape=out_shape,
        scratch_shapes=[pltpu.VMEM((tm, D_MODEL), BF16), pltpu.VMEM((tm, D_MODEL), F32)],
        compiler_params=_params(2),
        name="ffn",
    )(*args)
    return out if emit_h else out[0]


def _hnorm_kernel(x_ref, mod_ref, g_ref, o_ref):
    mod = mod_ref[0, 0]
    _store_grouped(o_ref, _norm_mod(x_ref[0], g_ref[0], mod[0:1], mod[1:2]).astype(BF16))


def _hnorm(x, mod, layer, b_off, norm_g):
    n_b, seq, _ = x.shape
    tm = min(FFN_TOKEN_TILE, seq)
    tile = pl.BlockSpec((1, tm, D_MODEL), lambda b, i: (b, i, 0))
    return pl.pallas_call(
        _hnorm_kernel,
        grid=(n_b, seq // tm),
        in_specs=[tile, _mod_spec(layer, b_off),
                  pl.BlockSpec((1, 1, D_MODEL), lambda b, i: (layer, 0, 0))],
        out_specs=_grouped_tile(tm),
        out_shape=_grouped_shape(n_b, seq),
        compiler_params=_params(2),
        name="mixer_input_norm",
    )(x, mod, norm_g)


def _fnet_dft_kernel(h_ref, wc_ref, wa_ref, mb_ref, o_ref, rg_scr, ab_scr, v_scr,
                     *, n2, tc, pitch, scale):
    seq = h_ref.shape[2]
    gd = FNET_GROUP_DIM
    k1l = DFT_LONG // n2
    nblk = gd // LANES
    for c in range(seq // tc):
        ab = jnp.dot(h_ref[0, 0, c * tc:(c + 1) * tc, :], wc_ref[...], preferred_element_type=F32)
        for part in range(2):
            for t in range(nblk):
                lo = part * gd + t * LANES
                if pitch == n2:
                    rg_scr[part * nblk + t, c * tc:(c + 1) * tc, :] = ab[:, lo:lo + LANES]
                else:
                    for q in range(tc // n2):
                        r0 = (c * (tc // n2) + q) * pitch
                        rg_scr[part * nblk + t, r0:r0 + n2, :] = ab[q * n2:(q + 1) * n2, lo:lo + LANES]
    for j in range(n2):
        for part in range(2):
            for t in range(nblk):
                sel = rg_scr[part * nblk + t, pl.ds(j, DFT_LONG, stride=pitch), :]
                ab_scr[part, j, :, t * LANES:(t + 1) * LANES] = sel.astype(BF16)
    for j in range(n2):
        ab = ab_scr[:, j].reshape(2 * DFT_LONG, gd)
        v = jnp.dot(wa_ref[j], ab, preferred_element_type=F32)
        v_scr[:, :, j] = v.reshape(2, n2, k1l, gd).astype(BF16)
    for kb in range(n2):
        vb = v_scr[:, kb].reshape(2 * DFT_LONG, gd)
        f = jnp.dot(mb_ref[...], vb, preferred_element_type=F32) * scale
        o_ref[0, 0, :, kb] = f.reshape(n2, k1l, gd).astype(BF16)


def _fnet_tables(seq):
    n2 = seq // DFT_LONG
    k1l = DFT_LONG // n2
    two_pi = 2.0 * math.pi
    gd = FNET_GROUP_DIM
    c = jnp.arange(gd, dtype=jnp.int32)
    ang = ((c[:, None] * c[None, :]) % gd).astype(F32) * (two_pi / gd)
    inv = 1.0 / math.sqrt(gd)
    wc = jnp.concatenate([jnp.cos(ang) * inv, jnp.sin(ang) * inv], axis=1).astype(BF16)

    k1 = jnp.arange(DFT_LONG, dtype=jnp.int32)
    n1 = jnp.arange(DFT_LONG, dtype=jnp.int32)
    j2 = jnp.arange(n2, dtype=jnp.int32)
    sp = n2 * n1[None, None, :] + j2[:, None, None]
    ang = ((k1[None, :, None] * sp) % seq).astype(F32) * (two_pi / seq)
    ca, sa = jnp.cos(ang), jnp.sin(ang)
    wa = jnp.concatenate([jnp.concatenate([ca, -sa], axis=2),
                          jnp.concatenate([sa, ca], axis=2)], axis=1).astype(BF16)

    ang = ((j2[:, None] * j2[None, :]) % n2).astype(F32) * (two_pi / n2)
    eye = jnp.eye(k1l, dtype=F32)
    cb = jnp.einsum("kn,ab->kanb", jnp.cos(ang), eye).reshape(DFT_LONG, DFT_LONG)
    sb = jnp.einsum("kn,ab->kanb", jnp.sin(ang), eye).reshape(DFT_LONG, DFT_LONG)
    mb = jnp.concatenate([cb, -sb], axis=1).astype(BF16)
    return wc, wa, mb


def _fnet_dft(h):
    n_b, _, seq, _ = h.shape
    n2 = seq // DFT_LONG
    k1l = DFT_LONG // n2
    tc = min(DFT_CHANNEL_TILE, seq)
    gd = FNET_GROUP_DIM
    wc, wa, mb = _fnet_tables(seq)
    pitch = n2 + SUBLANES // 2 if n2 % SUBLANES == 0 else n2
    out = pl.pallas_call(
        functools.partial(_fnet_dft_kernel, n2=n2, tc=tc, pitch=pitch, scale=1.0 / math.sqrt(seq)),
        grid=(n_b, FNET_GROUPS),
        in_specs=[
            pl.BlockSpec((1, 1, seq, gd), lambda b, g: (b, g, 0, 0)),
            pl.BlockSpec((gd, 2 * gd), lambda b, g: (0, 0)),
            _resident((n2, 2 * DFT_LONG, 2 * DFT_LONG), lambda b, g: (0, 0, 0)),
            pl.BlockSpec((DFT_LONG, 2 * DFT_LONG), lambda b, g: (0, 0)),
        ],
        out_specs=pl.BlockSpec((1, 1, n2, n2, k1l, gd), lambda b, g: (b, g, 0, 0, 0, 0)),
        out_shape=jax.ShapeDtypeStruct((n_b, FNET_GROUPS, n2, n2, k1l, gd), BF16),
        scratch_shapes=[
            pltpu.VMEM((2 * gd // LANES, DFT_LONG * pitch, LANES), F32),
            pltpu.VMEM((2, n2, DFT_LONG, gd), BF16),
            pltpu.VMEM((2, n2, n2, k1l, gd), BF16),
        ],
        compiler_params=_params(2),
        name="fnet_dft",
    )(h, wc, wa, mb)
    return out.reshape(n_b, FNET_GROUPS, seq, gd)


def _attn_qkv_kernel(x_ref, mod_ref, g_ref, w_ref, qg_ref, kg_ref, cos_ref, sin_ref,
                     q_ref, k_ref, v_ref):
    mod = mod_ref[0, 0]
    hd = HEAD_DIM
    k0 = N_HEADS * hd
    v0 = k0 + N_KV_HEADS * hd
    sub = SUB_TILE
    for st in range(x_ref.shape[1] // sub):
        r = slice(st * sub, (st + 1) * sub)
        h = _norm_mod(x_ref[0, r, :], g_ref[0], mod[0:1], mod[1:2]).astype(BF16)
        qkv = jnp.dot(h, w_ref[0], preferred_element_type=F32)
        cos = cos_ref[r, :]
        sin = sin_ref[r, :]

        def head(t, gain, scale):
            t = _rms(t) * gain
            return ((t * cos + pltpu.roll(t, hd // 2, 1) * sin) * scale).astype(BF16)

        for i in range(N_HEADS):
            q_ref[0, r, i * hd:(i + 1) * hd] = head(qkv[:, i * hd:(i + 1) * hd], qg_ref[0],
                                                     HEAD_DIM ** -0.5 * LOG2_E)
        ones = jnp.ones((sub, hd), BF16)
        for i in range(N_KV_HEADS):
            k_ref[0, r, i * hd:(i + 1) * hd] = head(qkv[:, k0 + i * hd:k0 + (i + 1) * hd],
                                                     kg_ref[0], 1.0)
            v_ref[0, r, 2 * i * hd:(2 * i + 1) * hd] = (
                qkv[:, v0 + i * hd:v0 + (i + 1) * hd].astype(BF16))
            v_ref[0, r, (2 * i + 1) * hd:(2 * i + 2) * hd] = ones


def _attn_kernel(q_ref, k_ref, v_ref, x_ref, mod_ref, wo_ref, o_ref, o_scr):
    hd = HEAD_DIM
    for g in range(N_KV_HEADS):
        k = k_ref[0, :, g * hd:(g + 1) * hd]
        v1 = v_ref[0, :, 2 * g * hd:(2 * g + 2) * hd]
        for i in range(Q_PER_KV):
            hh = g * Q_PER_KV + i
            q = q_ref[0, :, hh * hd:(hh + 1) * hd]
            s = lax.dot_general(q, k, (((1,), (1,)), ((), ())), preferred_element_type=F32)
            p = jnp.exp2(s - jnp.max(s, axis=-1, keepdims=True)).astype(BF16)
            pv = jnp.dot(p, v1, preferred_element_type=F32)
            o = pv[:, :hd] / pv[:, hd:hd + 1]
            o_scr[:, hh * hd:(hh + 1) * hd] = o.astype(BF16)
    m = jnp.dot(o_scr[...], wo_ref[0], preferred_element_type=F32)
    o_ref[0] = x_ref[0] + mod_ref[0, 0][2:3] * m


def _rope_lane_order():
    q = HEAD_DIM // 4
    return [*range(0, q), *range(2 * q, 3 * q), *range(q, 2 * q), *range(3 * q, 4 * q)]


def _rope_tables(seq):
    pos = jnp.arange(seq, dtype=jnp.int32)
    row = (pos // GRID_W).astype(F32)
    col = (pos % GRID_W).astype(F32)
    half = HEAD_DIM // 2
    freqs = 1.0 / (ROPE_THETA ** (jnp.arange(0, half, 2, dtype=F32) / half))
    ang_r = row[:, None] * freqs[None, :]
    ang_c = col[:, None] * freqs[None, :]
    cos = jnp.concatenate([jnp.cos(ang_r), jnp.cos(ang_c), jnp.cos(ang_r), jnp.cos(ang_c)], axis=1)
    sin = jnp.concatenate([-jnp.sin(ang_r), -jnp.sin(ang_c), jnp.sin(ang_r), jnp.sin(ang_c)], axis=1)
    return cos, sin


def _attention(x, mod, layer, b_off, norm_g, w_qkv, q_g, k_g, w_o, j):
    n_b, seq, _ = x.shape
    tm = min(TOKEN_TILE, seq)
    cos, sin = _rope_tables(seq)
    kv_dim = N_KV_HEADS * HEAD_DIM
    q, k, v = pl.pallas_call(
        _attn_qkv_kernel,
        grid=(n_b, seq // tm),
        in_specs=[
            pl.BlockSpec((1, tm, D_MODEL), lambda b, i: (b, i, 0)),
            _mod_spec(layer, b_off),
            pl.BlockSpec((1, 1, D_MODEL), lambda b, i: (layer, 0, 0)),
            pl.BlockSpec((1, D_MODEL, QKV_DIM), lambda b, i: (j, 0, 0)),
            pl.BlockSpec((1, 1, HEAD_DIM), lambda b, i: (j, 0, 0)),
            pl.BlockSpec((1, 1, HEAD_DIM), lambda b, i: (j, 0, 0)),
            pl.BlockSpec((tm, HEAD_DIM), lambda b, i: (i, 0)),
            pl.BlockSpec((tm, HEAD_DIM), lambda b, i: (i, 0)),
        ],
        out_specs=[
            pl.BlockSpec((1, tm, D_MODEL), lambda b, i: (b, i, 0)),
            pl.BlockSpec((1, tm, kv_dim), lambda b, i: (b, i, 0)),
            pl.BlockSpec((1, tm, 2 * kv_dim), lambda b, i: (b, i, 0)),
        ],
        out_shape=[
            jax.ShapeDtypeStruct((n_b, seq, D_MODEL), BF16),
            jax.ShapeDtypeStruct((n_b, seq, kv_dim), BF16),
            jax.ShapeDtypeStruct((n_b, seq, 2 * kv_dim), BF16),
        ],
        compiler_params=_params(2),
        name="attn_qkv_rope",
    )(x, mod, norm_g, w_qkv, q_g, k_g, cos, sin)

    tq = min(ATTN_Q_TILE, seq)
    return pl.pallas_call(
        _attn_kernel,
        grid=(n_b, seq // tq),
        in_specs=[
            pl.BlockSpec((1, tq, D_MODEL), lambda b, i: (b, i, 0)),
            pl.BlockSpec((1, seq, kv_dim), lambda b, i: (b, 0, 0)),
            pl.BlockSpec((1, seq, 2 * kv_dim), lambda b, i: (b, 0, 0)),
            pl.BlockSpec((1, tq, D_MODEL), lambda b, i: (b, i, 0)),
            _mod_spec(layer, b_off),
            pl.BlockSpec((1, D_MODEL, D_MODEL), lambda b, i: (j, 0, 0)),
        ],
        out_specs=pl.BlockSpec((1, tq, D_MODEL), lambda b, i: (b, i, 0)),
        out_shape=jax.ShapeDtypeStruct(x.shape, F32),
        scratch_shapes=[pltpu.VMEM((tq, D_MODEL), BF16)],
        compiler_params=_params(2),
        name="attn_softmax_out",
    )(q, k, v, x, mod, w_o)


def _sgu_kernel(x_ref, mod_ref, g_ref, win_ref, lng_ref, lnb_ref, ws_ref, bias_ref, wo_ref,
                o_ref, y_scr):
    mod = mod_ref[0, 0]
    ck = SGU_CHUNK
    gd = D_MODEL // SGU_GROUPS
    bias = bias_ref[0]
    sub = SGU_SUB_TILE
    n_sub = x_ref.shape[1] // sub

    def gate_inputs(st):
        x = x_ref[0, st * sub:(st + 1) * sub, :]
        h = _norm_mod(x, g_ref[0], mod[0:1], mod[1:2]).astype(BF16)
        uv = jnp.dot(h, win_ref[0], preferred_element_type=F32)
        uv = 0.5 * uv * (1.0 + lax.erf(uv * math.sqrt(0.5)))
        u = uv[:, :D_MODEL]
        v = uv[:, D_MODEL:]
        vc = v - jnp.mean(v, axis=-1, keepdims=True)
        vn = vc * lax.rsqrt(jnp.mean(vc * vc, axis=-1, keepdims=True) + EPS)
        return u, (vn * lng_ref[0] + lnb_ref[0]).astype(BF16)

    def gate_and_project(st, u, vn):
        r0 = st * sub
        n_ck = sub // ck
        for g in range(SGU_GROUPS):
            cols = slice(g * gd, (g + 1) * gd)
            rhs = jnp.concatenate([vn[c * ck:(c + 1) * ck, cols] for c in range(n_ck)], axis=1)
            sv = jnp.dot(ws_ref[0, g], rhs, preferred_element_type=F32)
            for c in range(n_ck):
                y_scr[r0 + c * ck:r0 + (c + 1) * ck, cols] = (
                    u[c * ck:(c + 1) * ck, cols]
                    * (sv[:, c * gd:(c + 1) * gd] + bias[:, cols])).astype(BF16)
        m = jnp.dot(y_scr[r0:r0 + sub, :], wo_ref[0], preferred_element_type=F32)
        o_ref[0, r0:r0 + sub, :] = x_ref[0, r0:r0 + sub, :] + mod[2:3] * m

    pending = gate_inputs(0)
    for st in range(1, n_sub):
        nxt = gate_inputs(st)
        gate_and_project(st - 1, *pending)
        pending = nxt
    gate_and_project(n_sub - 1, *pending)


def _sgu(x, mod, layer, b_off, norm_g, w_in, ln_g, ln_b, w_s, bias, w_o, j):
    n_b, seq, _ = x.shape
    tm = min(2 * SGU_SUB_TILE, seq)
    return pl.pallas_call(
        _sgu_kernel,
        grid=(n_b, seq // tm),
        in_specs=[
            pl.BlockSpec((1, tm, D_MODEL), lambda b, i: (b, i, 0)),
            _mod_spec(layer, b_off),
            pl.BlockSpec((1, 1, D_MODEL), lambda b, i: (layer, 0, 0)),
            pl.BlockSpec((1, D_MODEL, 2 * D_MODEL), lambda b, i: (j, 0, 0)),
            pl.BlockSpec((1, 1, D_MODEL), lambda b, i: (j, 0, 0)),
            pl.BlockSpec((1, 1, D_MODEL), lambda b, i: (j, 0, 0)),
            pl.BlockSpec((1, SGU_GROUPS, SGU_CHUNK, SGU_CHUNK), lambda b, i: (j, 0, 0, 0)),
            pl.BlockSpec((1, SGU_CHUNK, D_MODEL), lambda b, i: (j, 0, 0)),
            pl.BlockSpec((1, D_MODEL, D_MODEL), lambda b, i: (j, 0, 0)),
        ],
        out_specs=pl.BlockSpec((1, tm, D_MODEL), lambda b, i: (b, i, 0)),
        out_shape=jax.ShapeDtypeStruct(x.shape, F32),
        scratch_shapes=[pltpu.VMEM((tm, D_MODEL), BF16)],
        compiler_params=_params(2),
        name="sgu",
    )(x, mod, norm_g, w_in, ln_g, ln_b, w_s, bias, w_o)


def _trunk(x, mod, b_off, w):
    h_next = None
    for i in range(DEPTH):
        kind, j = i % 3, i // 3
        last = i == DEPTH - 1
        emit_h = (not last) and (i + 1) % 3 == 0
        proj = None
        if kind == 0:
            h = h_next if h_next is not None else _hnorm(x, mod, i, b_off, w["norm1_g"])
            proj = (_fnet_dft(h), w["fnet_w_o"], j)
        elif kind == 1:
            x = _attention(x, mod, i, b_off, w["norm1_g"], w["attn_w_qkv"], w["attn_q_g"],
                           w["attn_k_g"], w["attn_w_o"], j)
        else:
            x = _sgu(x, mod, i, b_off, w["norm1_g"], w["sgu_w_in"], w["sgu_ln_g"], w["sgu_ln_b"],
                     w["sgu_w_s"], w["sgu_bias"], w["sgu_w_o"], j)
        out = _ffn(x, mod, i, b_off, w, final=last, proj=proj, emit_h=emit_h)
        x, h_next = out if emit_h else (out, None)
    return x


def kernel(x_prompt, x_sample, c_prompt, c_sample, norm1_g, norm2_g, w_ada, b_ada, fnet_w_o,
           attn_w_qkv, attn_q_g, attn_k_g, attn_w_o, sgu_w_in, sgu_ln_g, sgu_ln_b, sgu_w_s, sgu_b_s,
           sgu_w_o, ffn_w_gu, ffn_w_down, final_g):
    depth = norm1_g.shape[0]
    gd = D_MODEL // SGU_GROUPS
    lane_order = np.asarray(_rope_lane_order())
    n_rot = (N_HEADS + N_KV_HEADS) * HEAD_DIM
    qkv_cols = np.concatenate([(np.arange(n_rot) // HEAD_DIM) * HEAD_DIM + np.tile(lane_order, n_rot // HEAD_DIM),
                               np.arange(n_rot, QKV_DIM)])
    sgu_bias = jnp.repeat(jnp.swapaxes(sgu_b_s, 1, 2), gd, axis=2)
    w = {
        "norm1_g": norm1_g.reshape(depth, 1, D_MODEL),
        "norm2_g": norm2_g.reshape(depth, 1, D_MODEL),
        "fnet_w_o": fnet_w_o.astype(BF16),
        "attn_w_qkv": attn_w_qkv[:, :, qkv_cols].astype(BF16),
        "attn_q_g": attn_q_g[:, lane_order].reshape(-1, 1, HEAD_DIM),
        "attn_k_g": attn_k_g[:, lane_order].reshape(-1, 1, HEAD_DIM),
        "attn_w_o": attn_w_o.astype(BF16),
        "sgu_w_in": sgu_w_in.astype(BF16),
        "sgu_ln_g": sgu_ln_g.reshape(-1, 1, D_MODEL),
        "sgu_ln_b": sgu_ln_b.reshape(-1, 1, D_MODEL),
        "sgu_w_s": sgu_w_s.astype(BF16),
        "sgu_bias": sgu_bias,
        "sgu_w_o": sgu_w_o.astype(BF16),
        "ffn_wgu": ffn_w_gu.astype(BF16),
        "ffn_wd": ffn_w_down.astype(BF16),
        "final_g": final_g.reshape(1, D_MODEL),
    }
    c_all = jnp.concatenate([c_prompt, c_sample], axis=0)
    mod = _modulation(c_all, w_ada, b_ada)
    y_prompt = _trunk(x_prompt, mod, 0, w)
    y_sample = _trunk(x_sample, mod, x_prompt.shape[0], w)
    return (y_prompt, y_sample)
```

```python
import functools
import math

import numpy as np

import jax
import jax.numpy as jnp
from jax import lax
from jax.experimental import pallas as pl
from jax.experimental.pallas import tpu as pltpu

F32 = jnp.float32
BF16 = jnp.bfloat16

D_MODEL = 1024
DEPTH = 4
EPS = 1e-6
GRID_W = 64
FNET_GROUPS = 4
FNET_GROUP_DIM = D_MODEL // FNET_GROUPS
HEAD_DIM = 128
N_HEADS = 8
N_KV_HEADS = 2
Q_PER_KV = N_HEADS // N_KV_HEADS
QKV_DIM = (N_HEADS + 2 * N_KV_HEADS) * HEAD_DIM
ROPE_THETA = 10000.0
SGU_GROUPS = 8
SGU_CHUNK = 128
FFN_HIDDEN = 2816

DFT_LONG = 256
LANES = 128
SUBLANES = 8
TOKEN_TILE = 512
FFN_TOKEN_TILE = 1024
LOG2_E = 1.4426950408889634
ATTN_Q_TILE = 512
SUB_TILE = 256
SGU_SUB_TILE = 512
DFT_CHANNEL_TILE = 2048
FFN_CHUNK = 256
VMEM_LIMIT = 56 * 1024 * 1024


def _params(n_axes):
    return pltpu.CompilerParams(
        dimension_semantics=("parallel",) * n_axes, vmem_limit_bytes=VMEM_LIMIT)


def _resident(block_shape, index_map):
    return pl.BlockSpec(block_shape, index_map, pipeline_mode=pl.Buffered(1))


def _rms(x):
    return x * lax.rsqrt(jnp.mean(x * x, axis=-1, keepdims=True) + EPS)


def _norm_mod(x, gain, shift, scale):
    return _rms(x) * (gain * (1.0 + scale)) + shift


def _mod_kernel(c_ref, w_ref, b_ref, o_ref):
    c = c_ref[...]
    cs = c * jax.nn.sigmoid(c)
    o_ref[0] = jnp.dot(cs, w_ref[0], precision=lax.Precision.HIGHEST,
                       preferred_element_type=F32) + b_ref[0]


def _modulation(c_all, w_ada, b_ada):
    n_b = c_all.shape[0]
    tn = 1536
    out = pl.pallas_call(
        _mod_kernel,
        grid=(DEPTH, 6 * D_MODEL // tn),
        in_specs=[
            pl.BlockSpec((n_b, D_MODEL), lambda l, j: (0, 0)),
            pl.BlockSpec((1, D_MODEL, tn), lambda l, j: (l, 0, j)),
            pl.BlockSpec((1, 1, tn), lambda l, j: (l, 0, j)),
        ],
        out_specs=pl.BlockSpec((1, n_b, tn), lambda l, j: (l, 0, j)),
        out_shape=jax.ShapeDtypeStruct((DEPTH, n_b, 6 * D_MODEL), F32),
        compiler_params=_params(2),
        name="adaln_mod",
    )(c_all, w_ada, b_ada.reshape(DEPTH, 1, 6 * D_MODEL))
    return out.reshape(DEPTH, n_b, 6, D_MODEL)


def _mod_spec(layer, b_off):
    return pl.BlockSpec((1, 1, 6, D_MODEL), lambda b, i: (layer, b_off + b, 0, 0))


def _grouped_shape(n_b, seq):
    return jax.ShapeDtypeStruct((n_b, FNET_GROUPS, seq, FNET_GROUP_DIM), BF16)


def _grouped_tile(tm):
    return pl.BlockSpec((1, FNET_GROUPS, tm, FNET_GROUP_DIM), lambda b, i: (b, 0, i, 0))


def _store_grouped(ref, value):
    for g in range(FNET_GROUPS):
        ref[0, g] = value[:, g * FNET_GROUP_DIM:(g + 1) * FNET_GROUP_DIM]


def _ffn_kernel(*refs, n_chunks, chunk, final, proj, emit_h):
    refs = list(refs)
    x_ref, mod_ref, g_ref, wgu_ref, wd_ref, fg_ref = refs[:6]
    pos = 6
    if proj:
        f_ref, wo_ref = refs[pos:pos + 2]
        pos += 2
    if emit_h:
        modn_ref, gn_ref = refs[pos:pos + 2]
        pos += 2
    o_ref = refs[pos]
    pos += 1
    if emit_h:
        hn_ref = refs[pos]
        pos += 1
    h_scr, acc_scr = refs[pos:pos + 2]

    mod = mod_ref[0, 0]
    if proj:
        f = jnp.concatenate([f_ref[0, g] for g in range(FNET_GROUPS)], axis=-1)
        m = jnp.dot(f, wo_ref[0], preferred_element_type=F32)
        o_ref[0] = x_ref[0] + mod[2:3] * m
        res_ref = o_ref
    else:
        res_ref = x_ref
    h_scr[...] = _norm_mod(res_ref[0], g_ref[0], mod[3:4], mod[4:5]).astype(BF16)
    acc_scr[...] = jnp.zeros_like(acc_scr)

    hidden = n_chunks * chunk
    for j in range(n_chunks):
        lo = j * chunk
        g = jnp.dot(h_scr[...], wgu_ref[0, :, lo:lo + chunk], preferred_element_type=F32)
        u = jnp.dot(h_scr[...], wgu_ref[0, :, hidden + lo:hidden + lo + chunk],
                    preferred_element_type=F32)
        a = (g * jax.nn.sigmoid(g) * u).astype(BF16)
        acc_scr[...] += jnp.dot(a, wd_ref[0, lo:lo + chunk, :], preferred_element_type=F32)
    y = res_ref[0] + mod[5:6] * acc_scr[...]
    if emit_h:
        modn = modn_ref[0, 0]
        _store_grouped(hn_ref, _norm_mod(y, gn_ref[0], modn[0:1], modn[1:2]).astype(BF16))
    if final:
        y = _rms(y) * fg_ref[...]
    o_ref[0] = y


def _ffn(x, mod, layer, b_off, w, *, final=False, proj=None, emit_h=False):
    n_b, seq, _ = x.shape
    tm = min(FFN_TOKEN_TILE, seq)
    wgu, wd = w["ffn_wgu"], w["ffn_wd"]
    hidden = wd.shape[1]
    chunk = FFN_CHUNK
    n_chunks = hidden // chunk
    tile = pl.BlockSpec((1, tm, D_MODEL), lambda b, i: (b, i, 0))
    args = [x, mod, w["norm2_g"], wgu, wd, w["final_g"]]
    in_specs = [
        tile,
        _mod_spec(layer, b_off),
        pl.BlockSpec((1, 1, D_MODEL), lambda b, i: (layer, 0, 0)),
        _resident((1, D_MODEL, 2 * hidden), lambda b, i: (layer, 0, 0)),
        _resident((1, hidden, D_MODEL), lambda b, i: (layer, 0, 0)),
        pl.BlockSpec((1, D_MODEL), lambda b, i: (0, 0)),
    ]
    if proj is not None:
        f, w_o, j = proj
        args += [f, w_o]
        in_specs += [_grouped_tile(tm), _resident((1, D_MODEL, D_MODEL), lambda b, i: (j, 0, 0))]
    out_shape = [jax.ShapeDtypeStruct(x.shape, F32)]
    out_specs = [tile]
    if emit_h:
        args += [mod, w["norm1_g"]]
        in_specs += [_mod_spec(layer + 1, b_off),
                     pl.BlockSpec((1, 1, D_MODEL), lambda b, i: (layer + 1, 0, 0))]
        out_shape.append(_grouped_shape(n_b, seq))
        out_specs.append(_grouped_tile(tm))
    kern = functools.partial(_ffn_kernel, n_chunks=n_chunks, chunk=chunk, final=final,
                             proj=proj is not None, emit_h=emit_h)
    out = pl.pallas_call(
        kern,
        grid=(n_b, seq // tm),
        in_specs=in_specs,
        out_specs=out_specs,
        out_shape=out_shape,
        scratch_shapes=[pltpu.VMEM((tm, D_MODEL), BF16), pltpu.VMEM((tm, D_MODEL), F32)],
        compiler_params=_params(2),
        name="ffn",
    )(*args)
    return out if emit_h else out[0]


def _hnorm_kernel(x_ref, mod_ref, g_ref, o_ref):
    mod = mod_ref[0, 0]
    _store_grouped(o_ref, _norm_mod(x_ref[0], g_ref[0], mod[0:1], mod[1:2]).astype(BF16))


def _hnorm(x, mod, layer, b_off, norm_g):
    n_b, seq, _ = x.shape
    tm = min(FFN_TOKEN_TILE, seq)
    tile = pl.BlockSpec((1, tm, D_MODEL), lambda b, i: (b, i, 0))
    return pl.pallas_call(
        _hnorm_kernel,
        grid=(n_b, seq // tm),
        in_specs=[tile, _mod_spec(layer, b_off),
                  pl.BlockSpec((1, 1, D_MODEL), lambda b, i: (layer, 0, 0))],
        out_specs=_grouped_tile(tm),
        out_shape=_grouped_shape(n_b, seq),
        compiler_params=_params(2),
        name="mixer_input_norm",
    )(x, mod, norm_g)


def _fnet_dft_kernel(h_ref, wc_ref, wa_ref, mb_ref, o_ref, rg_scr, ab_scr, v_scr,
                     *, n2, tc, pitch, scale):
    seq = h_ref.shape[2]
    gd = FNET_GROUP_DIM
    k1l = DFT_LONG // n2
    nblk = gd // LANES
    for c in range(seq // tc):
        ab = jnp.dot(h_ref[0, 0, c * tc:(c + 1) * tc, :], wc_ref[...], preferred_element_type=F32)
        for part in range(2):
            for t in range(nblk):
                lo = part * gd + t * LANES
                if pitch == n2:
                    rg_scr[part * nblk + t, c * tc:(c + 1) * tc, :] = ab[:, lo:lo + LANES]
                else:
                    for q in range(tc // n2):
                        r0 = (c * (tc // n2) + q) * pitch
                        rg_scr[part * nblk + t, r0:r0 + n2, :] = ab[q * n2:(q + 1) * n2, lo:lo + LANES]
    for j in range(n2):
        for part in range(2):
            for t in range(nblk):
                sel = rg_scr[part * nblk + t, pl.ds(j, DFT_LONG, stride=pitch), :]
                ab_scr[part, j, :, t * LANES:(t + 1) * LANES] = sel.astype(BF16)
    for j in range(n2):
        ab = ab_scr[:, j].reshape(2 * DFT_LONG, gd)
        v = jnp.dot(wa_ref[j], ab, preferred_element_type=F32)
        v_scr[:, :, j] = v.reshape(2, n2, k1l, gd).astype(BF16)
    for kb in range(n2):
        vb = v_scr[:, kb].reshape(2 * DFT_LONG, gd)
        f = jnp.dot(mb_ref[...], vb, preferred_element_type=F32) * scale
        o_ref[0, 0, :, kb] = f.reshape(n2, k1l, gd).astype(BF16)


def _fnet_tables(seq):
    n2 = seq // DFT_LONG
    k1l = DFT_LONG // n2
    two_pi = 2.0 * math.pi
    gd = FNET_GROUP_DIM
    c = jnp.arange(gd, dtype=jnp.int32)
    ang = ((c[:, None] * c[None, :]) % gd).astype(F32) * (two_pi / gd)
    inv = 1.0 / math.sqrt(gd)
    wc = jnp.concatenate([jnp.cos(ang) * inv, jnp.sin(ang) * inv], axis=1).astype(BF16)

    k1 = jnp.arange(DFT_LONG, dtype=jnp.int32)
    n1 = jnp.arange(DFT_LONG, dtype=jnp.int32)
    j2 = jnp.arange(n2, dtype=jnp.int32)
    sp = n2 * n1[None, None, :] + j2[:, None, None]
    ang = ((k1[None, :, None] * sp) % seq).astype(F32) * (two_pi / seq)
    ca, sa = jnp.cos(ang), jnp.sin(ang)
    wa = jnp.concatenate([jnp.concatenate([ca, -sa], axis=2),
                          jnp.concatenate([sa, ca], axis=2)], axis=1).astype(BF16)

    ang = ((j2[:, None] * j2[None, :]) % n2).astype(F32) * (two_pi / n2)
    eye = jnp.eye(k1l, dtype=F32)
    cb = jnp.einsum("kn,ab->kanb", jnp.cos(ang), eye).reshape(DFT_LONG, DFT_LONG)
    sb = jnp.einsum("kn,ab->kanb", jnp.sin(ang), eye).reshape(DFT_LONG, DFT_LONG)
    mb = jnp.concatenate([cb, -sb], axis=1).astype(BF16)
    return wc, wa, mb


def _fnet_dft(h):
    n_b, _, seq, _ = h.shape
    n2 = seq // DFT_LONG
    k1l = DFT_LONG // n2
    tc = min(DFT_CHANNEL_TILE, seq)
    gd = FNET_GROUP_DIM
    wc, wa, mb = _fnet_tables(seq)
    pitch = n2 + SUBLANES // 2 if n2 % SUBLANES == 0 else n2
    out = pl.pallas_call(
        functools.partial(_fnet_dft_kernel, n2=n2, tc=tc, pitch=pitch, scale=1.0 / math.sqrt(seq)),
        grid=(n_b, FNET_GROUPS),
        in_specs=[
            pl.BlockSpec((1, 1, seq, gd), lambda b, g: (b, g, 0, 0)),
            pl.BlockSpec((gd, 2 * gd), lambda b, g: (0, 0)),
            _resident((n2, 2 * DFT_LONG, 2 * DFT_LONG), lambda b, g: (0, 0, 0)),
            pl.BlockSpec((DFT_LONG, 2 * DFT_LONG), lambda b, g: (0, 0)),
        ],
        out_specs=pl.BlockSpec((1, 1, n2, n2, k1l, gd), lambda b, g: (b, g, 0, 0, 0, 0)),
        out_shape=jax.ShapeDtypeStruct((n_b, FNET_GROUPS, n2, n2, k1l, gd), BF16),
        scratch_shapes=[
            pltpu.VMEM((2 * gd // LANES, DFT_LONG * pitch, LANES), F32),
            pltpu.VMEM((2, n2, DFT_LONG, gd), BF16),
            pltpu.VMEM((2, n2, n2, k1l, gd), BF16),
        ],
        compiler_params=_params(2),
        name="fnet_dft",
    )(h, wc, wa, mb)
    return out.reshape(n_b, FNET_GROUPS, seq, gd)


def _attn_qkv_kernel(x_ref, mod_ref, g_ref, w_ref, qg_ref, kg_ref, cos_ref, sin_ref,
                     q_ref, k_ref, v_ref):
    mod = mod_ref[0, 0]
    hd = HEAD_DIM
    k0 = N_HEADS * hd
    v0 = k0 + N_KV_HEADS * hd
    sub = SUB_TILE
    for st in range(x_ref.shape[1] // sub):
        r = slice(st * sub, (st + 1) * sub)
        h = _norm_mod(x_ref[0, r, :], g_ref[0], mod[0:1], mod[1:2]).astype(BF16)
        qkv = jnp.dot(h, w_ref[0], preferred_element_type=F32)
        cos = cos_ref[r, :]
        sin = sin_ref[r, :]

        def head(t, gain, scale):
            t = _rms(t) * gain
            return ((t * cos + pltpu.roll(t, hd // 2, 1) * sin) * scale).astype(BF16)

        for i in range(N_HEADS):
            q_ref[0, r, i * hd:(i + 1) * hd] = head(qkv[:, i * hd:(i + 1) * hd], qg_ref[0],
                                                     HEAD_DIM ** -0.5 * LOG2_E)
        ones = jnp.ones((sub, hd), BF16)
        for i in range(N_KV_HEADS):
            k_ref[0, r, i * hd:(i + 1) * hd] = head(qkv[:, k0 + i * hd:k0 + (i + 1) * hd],
                                                     kg_ref[0], 1.0)
            v_ref[0, r, 2 * i * hd:(2 * i + 1) * hd] = (
                qkv[:, v0 + i * hd:v0 + (i + 1) * hd].astype(BF16))
            v_ref[0, r, (2 * i + 1) * hd:(2 * i + 2) * hd] = ones


def _attn_kernel(q_ref, k_ref, v_ref, o_ref):
    hd = HEAD_DIM
    per_group = FNET_GROUP_DIM // hd
    for g in range(N_KV_HEADS):
        k = k_ref[0, :, g * hd:(g + 1) * hd]
        v1 = v_ref[0, :, 2 * g * hd:(2 * g + 2) * hd]
        for i in range(Q_PER_KV):
            hh = g * Q_PER_KV + i
            q = q_ref[0, :, hh * hd:(hh + 1) * hd]
            s = lax.dot_general(q, k, (((1,), (1,)), ((), ())), preferred_element_type=F32)
            p = jnp.exp2(s - jnp.max(s, axis=-1, keepdims=True)).astype(BF16)
            pv = jnp.dot(p, v1, preferred_element_type=F32)
            o = pv[:, :hd] / pv[:, hd:hd + 1]
            lo = (hh % per_group) * hd
            o_ref[0, hh // per_group, :, lo:lo + hd] = o.astype(BF16)


def _rope_lane_order():
    q = HEAD_DIM // 4
    return [*range(0, q), *range(2 * q, 3 * q), *range(q, 2 * q), *range(3 * q, 4 * q)]


def _rope_tables(seq):
    pos = jnp.arange(seq, dtype=jnp.int32)
    row = (pos // GRID_W).astype(F32)
    col = (pos % GRID_W).astype(F32)
    half = HEAD_DIM // 2
    freqs = 1.0 / (ROPE_THETA ** (jnp.arange(0, half, 2, dtype=F32) / half))
    ang_r = row[:, None] * freqs[None, :]
    ang_c = col[:, None] * freqs[None, :]
    cos = jnp.concatenate([jnp.cos(ang_r), jnp.cos(ang_c), jnp.cos(ang_r), jnp.cos(ang_c)], axis=1)
    sin = jnp.concatenate([-jnp.sin(ang_r), -jnp.sin(ang_c), jnp.sin(ang_r), jnp.sin(ang_c)], axis=1)
    return cos, sin


def _attention(x, mod, layer, b_off, norm_g, w_qkv, q_g, k_g, j):
    n_b, seq, _ = x.shape
    tm = min(TOKEN_TILE, seq)
    cos, sin = _rope_tables(seq)
    kv_dim = N_KV_HEADS * HEAD_DIM
    q, k, v = pl.pallas_call(
        _attn_qkv_kernel,
        grid=(n_b, seq // tm),
        in_specs=[
            pl.BlockSpec((1, tm, D_MODEL), lambda b, i: (b, i, 0)),
            _mod_spec(layer, b_off),
            pl.BlockSpec((1, 1, D_MODEL), lambda b, i: (layer, 0, 0)),
            pl.BlockSpec((1, D_MODEL, QKV_DIM), lambda b, i: (j, 0, 0)),
            pl.BlockSpec((1, 1, HEAD_DIM), lambda b, i: (j, 0, 0)),
            pl.BlockSpec((1, 1, HEAD_DIM), lambda b, i: (j, 0, 0)),
            pl.BlockSpec((tm, HEAD_DIM), lambda b, i: (i, 0)),
            pl.BlockSpec((tm, HEAD_DIM), lambda b, i: (i, 0)),
        ],
        out_specs=[
            pl.BlockSpec((1, tm, D_MODEL), lambda b, i: (b, i, 0)),
            pl.BlockSpec((1, tm, kv_dim), lambda b, i: (b, i, 0)),
            pl.BlockSpec((1, tm, 2 * kv_dim), lambda b, i: (b, i, 0)),
        ],
        out_shape=[
            jax.ShapeDtypeStruct((n_b, seq, D_MODEL), BF16),
            jax.ShapeDtypeStruct((n_b, seq, kv_dim), BF16),
            jax.ShapeDtypeStruct((n_b, seq, 2 * kv_dim), BF16),
        ],
        compiler_params=_params(2),
        name="attn_qkv_rope",
    )(x, mod, norm_g, w_qkv, q_g, k_g, cos, sin)

    tq = min(ATTN_Q_TILE, seq)
    return pl.pallas_call(
        _attn_kernel,
        grid=(n_b, seq // tq),
        in_specs=[
            pl.BlockSpec((1, tq, D_MODEL), lambda b, i: (b, i, 0)),
            pl.BlockSpec((1, seq, kv_dim), lambda b, i: (b, 0, 0)),
            pl.BlockSpec((1, seq, 2 * kv_dim), lambda b, i: (b, 0, 0)),
        ],
        out_specs=_grouped_tile(tq),
        out_shape=_grouped_shape(n_b, seq),
        compiler_params=_params(2),
        name="attn_softmax_out",
    )(q, k, v)


def _sgu_kernel(x_ref, mod_ref, g_ref, win_ref, lng_ref, lnb_ref, ws_ref, bias_ref, wo_ref,
                o_ref, y_scr):
    mod = mod_ref[0, 0]
    ck = SGU_CHUNK
    gd = D_MODEL // SGU_GROUPS
    bias = bias_ref[0]
    sub = SGU_SUB_TILE
    n_sub = x_ref.shape[1] // sub

    def gate_inputs(st):
        x = x_ref[0, st * sub:(st + 1) * sub, :]
        h = _norm_mod(x, g_ref[0], mod[0:1], mod[1:2]).astype(BF16)
        uv = jnp.dot(h, win_ref[0], preferred_element_type=F32)
        uv = 0.5 * uv * (1.0 + lax.erf(uv * math.sqrt(0.5)))
        u = uv[:, :D_MODEL]
        v = uv[:, D_MODEL:]
        vc = v - jnp.mean(v, axis=-1, keepdims=True)
        vn = vc * lax.rsqrt(jnp.mean(vc * vc, axis=-1, keepdims=True) + EPS)
        return u, (vn * lng_ref[0] + lnb_ref[0]).astype(BF16)

    def gate_and_project(st, u, vn):
        r0 = st * sub
        n_ck = sub // ck
        for g in range(SGU_GROUPS):
            cols = slice(g * gd, (g + 1) * gd)
            rhs = jnp.concatenate([vn[c * ck:(c + 1) * ck, cols] for c in range(n_ck)], axis=1)
            sv = jnp.dot(ws_ref[0, g], rhs, preferred_element_type=F32)
            for c in range(n_ck):
                y_scr[r0 + c * ck:r0 + (c + 1) * ck, cols] = (
                    u[c * ck:(c + 1) * ck, cols]
                    * (sv[:, c * gd:(c + 1) * gd] + bias[:, cols])).astype(BF16)
        m = jnp.dot(y_scr[r0:r0 + sub, :], wo_ref[0], preferred_element_type=F32)
        o_ref[0, r0:r0 + sub, :] = x_ref[0, r0:r0 + sub, :] + mod[2:3] * m

    pending = gate_inputs(0)
    for st in range(1, n_sub):
        nxt = gate_inputs(st)
        gate_and_project(st - 1, *pending)
        pending = nxt
    gate_and_project(n_sub - 1, *pending)


def _sgu(x, mod, layer, b_off, norm_g, w_in, ln_g, ln_b, w_s, bias, w_o, j):
    n_b, seq, _ = x.shape
    tm = min(2 * SGU_SUB_TILE, seq)
    return pl.pallas_call(
        _sgu_kernel,
        grid=(n_b, seq // tm),
        in_specs=[
            pl.BlockSpec((1, tm, D_MODEL), lambda b, i: (b, i, 0)),
            _mod_spec(layer, b_off),
            pl.BlockSpec((1, 1, D_MODEL), lambda b, i: (layer, 0, 0)),
            pl.BlockSpec((1, D_MODEL, 2 * D_MODEL), lambda b, i: (j, 0, 0)),
            pl.BlockSpec((1, 1, D_MODEL), lambda b, i: (j, 0, 0)),
            pl.BlockSpec((1, 1, D_MODEL), lambda b, i: (j, 0, 0)),
            pl.BlockSpec((1, SGU_GROUPS, SGU_CHUNK, SGU_CHUNK), lambda b, i: (j, 0, 0, 0)),
            pl.BlockSpec((1, SGU_CHUNK, D_MODEL), lambda b, i: (j, 0, 0)),
            pl.BlockSpec((1, D_MODEL, D_MODEL), lambda b, i: (j, 0, 0)),
        ],
        out_specs=pl.BlockSpec((1, tm, D_MODEL), lambda b, i: (b, i, 0)),
        out_shape=jax.ShapeDtypeStruct(x.shape, F32),
        scratch_shapes=[pltpu.VMEM((tm, D_MODEL), BF16)],
        compiler_params=_params(2),
        name="sgu",
    )(x, mod, norm_g, w_in, ln_g, ln_b, w_s, bias, w_o)


def _trunk(x, mod, b_off, w):
    h_next = None
    for i in range(DEPTH):
        kind, j = i % 3, i // 3
        last = i == DEPTH - 1
        emit_h = (not last) and (i + 1) % 3 == 0
        proj = None
        if kind == 0:
            h = h_next if h_next is not None else _hnorm(x, mod, i, b_off, w["norm1_g"])
            proj = (_fnet_dft(h), w["fnet_w_o"], j)
        elif kind == 1:
            o = _attention(x, mod, i, b_off, w["norm1_g"], w["attn_w_qkv"], w["attn_q_g"],
                           w["attn_k_g"], j)
            proj = (o, w["attn_w_o"], j)
        else:
            x = _sgu(x, mod, i, b_off, w["norm1_g"], w["sgu_w_in"], w["sgu_ln_g"], w["sgu_ln_b"],
                     w["sgu_w_s"], w["sgu_bias"], w["sgu_w_o"], j)
        out = _ffn(x, mod, i, b_off, w, final=last, proj=proj, emit_h=emit_h)
        x, h_next = out if emit_h else (out, None)
    return x


def kernel(x_prompt, x_sample, c_prompt, c_sample, norm1_g, norm2_g, w_ada, b_ada, fnet_w_o,
           attn_w_qkv, attn_q_g, attn_k_g, attn_w_o, sgu_w_in, sgu_ln_g, sgu_ln_b, sgu_w_s, sgu_b_s,
           sgu_w_o, ffn_w_gu, ffn_w_down, final_g):
    depth = norm1_g.shape[0]
    gd = D_MODEL // SGU_GROUPS
    lane_order = np.asarray(_rope_lane_order())
    n_rot = (N_HEADS + N_KV_HEADS) * HEAD_DIM
    qkv_cols = np.concatenate([(np.arange(n_rot) // HEAD_DIM) * HEAD_DIM + np.tile(lane_order, n_rot // HEAD_DIM),
                               np.arange(n_rot, QKV_DIM)])
    sgu_bias = jnp.repeat(jnp.swapaxes(sgu_b_s, 1, 2), gd, axis=2)
    w = {
        "norm1_g": norm1_g.reshape(depth, 1, D_MODEL),
        "norm2_g": norm2_g.reshape(depth, 1, D_MODEL),
        "fnet_w_o": fnet_w_o.astype(BF16),
        "attn_w_qkv": attn_w_qkv[:, :, qkv_cols].astype(BF16),
        "attn_q_g": attn_q_g[:, lane_order].reshape(-1, 1, HEAD_DIM),
        "attn_k_g": attn_k_g[:, lane_order].reshape(-1, 1, HEAD_DIM),
        "attn_w_o": attn_w_o.astype(BF16),
        "sgu_w_in": sgu_w_in.astype(BF16),
        "sgu_ln_g": sgu_ln_g.reshape(-1, 1, D_MODEL),
        "sgu_ln_b": sgu_ln_b.reshape(-1, 1, D_MODEL),
        "sgu_w_s": sgu_w_s.astype(BF16),
        "sgu_bias": sgu_bias,
        "sgu_w_o": sgu_w_o.astype(BF16),
        "ffn_wgu": ffn_w_gu.astype(BF16),
        "ffn_wd": ffn_w_down.astype(BF16),
        "final_g": final_g.reshape(1, D_MODEL),
    }
    c_all = jnp.concatenate([c_prompt, c_sample], axis=0)
    mod = _modulation(c_all, w_ada, b_ada)
    y_prompt = _trunk(x_prompt, mod, 0, w)
    y_sample = _trunk(x_sample, mod, x_prompt.shape[0], w)
    return (y_prompt, y_sample)
```

```python
import functools
import math

import numpy as np

import jax
import jax.numpy as jnp
from jax import lax
from jax.experimental import pallas as pl
from jax.experimental.pallas import tpu as pltpu

F32 = jnp.float32
BF16 = jnp.bfloat16

D_MODEL = 1024
DEPTH = 4
EPS = 1e-6
GRID_W = 64
FNET_GROUPS = 4
FNET_GROUP_DIM = D_MODEL // FNET_GROUPS
HEAD_DIM = 128
N_HEADS = 8
N_KV_HEADS = 2
Q_PER_KV = N_HEADS // N_KV_HEADS
QKV_DIM = (N_HEADS + 2 * N_KV_HEADS) * HEAD_DIM
ROPE_THETA = 10000.0
SGU_GROUPS = 8
SGU_CHUNK = 128
FFN_HIDDEN = 2816

DFT_LONG = 256
LANES = 128
SUBLANES = 8
TOKEN_TILE = 512
FFN_TOKEN_TILE = 1024
LOG2_E = 1.4426950408889634
ATTN_Q_TILE = 512
SUB_TILE = 256
SGU_SUB_TILE = 512
DFT_CHANNEL_TILE = 2048
FFN_CHUNK = 256
VMEM_LIMIT = 56 * 1024 * 1024


def _params(n_axes):
    return pltpu.CompilerParams(
        dimension_semantics=("parallel",) * n_axes, vmem_limit_bytes=VMEM_LIMIT)


def _resident(block_shape, index_map):
    return pl.BlockSpec(block_shape, index_map, pipeline_mode=pl.Buffered(1))


def _rms(x):
    return x * lax.rsqrt(jnp.mean(x * x, axis=-1, keepdims=True) + EPS)


def _norm_mod(x, gain, shift, scale):
    return _rms(x) * (gain * (1.0 + scale)) + shift


def _mod_kernel(c_ref, w_ref, b_ref, o_ref):
    c = c_ref[...]
    cs = c * jax.nn.sigmoid(c)
    o_ref[0] = jnp.dot(cs, w_ref[0], precision=lax.Precision.HIGHEST,
                       preferred_element_type=F32) + b_ref[0]


def _modulation(c_all, w_ada, b_ada):
    n_b = c_all.shape[0]
    tn = 3072
    out = pl.pallas_call(
        _mod_kernel,
        grid=(DEPTH, 6 * D_MODEL // tn),
        in_specs=[
            pl.BlockSpec((n_b, D_MODEL), lambda l, j: (0, 0)),
            pl.BlockSpec((1, D_MODEL, tn), lambda l, j: (l, 0, j)),
            pl.BlockSpec((1, 1, tn), lambda l, j: (l, 0, j)),
        ],
        out_specs=pl.BlockSpec((1, n_b, tn), lambda l, j: (l, 0, j)),
        out_shape=jax.ShapeDtypeStruct((DEPTH, n_b, 6 * D_MODEL), F32),
        compiler_params=_params(2),
        name="adaln_mod",
    )(c_all, w_ada, b_ada.reshape(DEPTH, 1, 6 * D_MODEL))
    return out.reshape(DEPTH, n_b, 6, D_MODEL)


def _mod_spec(layer, b_off):
    return pl.BlockSpec((1, 1, 6, D_MODEL), lambda b, i: (layer, b_off + b, 0, 0))


def _grouped_shape(n_b, seq):
    return jax.ShapeDtypeStruct((n_b, FNET_GROUPS, seq, FNET_GROUP_DIM), BF16)


def _grouped_tile(tm):
    return pl.BlockSpec((1, FNET_GROUPS, tm, FNET_GROUP_DIM), lambda b, i: (b, 0, i, 0))


def _store_grouped(ref, value):
    for g in range(FNET_GROUPS):
        ref[0, g] = value[:, g * FNET_GROUP_DIM:(g + 1) * FNET_GROUP_DIM]


def _ffn_kernel(*refs, n_chunks, chunk, final, proj, emit_h):
    refs = list(refs)
    x_ref, mod_ref, g_ref, wgu_ref, wd_ref, fg_ref = refs[:6]
    pos = 6
    if proj:
        f_ref, wo_ref = refs[pos:pos + 2]
        pos += 2
    if emit_h:
        modn_ref, gn_ref = refs[pos:pos + 2]
        pos += 2
    o_ref = refs[pos]
    pos += 1
    if emit_h:
        hn_ref = refs[pos]
        pos += 1
    h_scr, acc_scr = refs[pos:pos + 2]

    mod = mod_ref[0, 0]
    if proj:
        f = jnp.concatenate([f_ref[0, g] for g in range(FNET_GROUPS)], axis=-1)
        m = jnp.dot(f, wo_ref[0], preferred_element_type=F32)
        o_ref[0] = x_ref[0] + mod[2:3] * m
        res_ref = o_ref
    else:
        res_ref = x_ref
    h_scr[...] = _norm_mod(res_ref[0], g_ref[0], mod[3:4], mod[4:5]).astype(BF16)
    acc_scr[...] = jnp.zeros_like(acc_scr)

    hidden = n_chunks * chunk
    for j in range(n_chunks):
        lo = j * chunk
        g = jnp.dot(h_scr[...], wgu_ref[0, :, lo:lo + chunk], preferred_element_type=F32)
        u = jnp.dot(h_scr[...], wgu_ref[0, :, hidden + lo:hidden + lo + chunk],
                    preferred_element_type=F32)
        a = (g * jax.nn.sigmoid(g) * u).astype(BF16)
        acc_scr[...] += jnp.dot(a, wd_ref[0, lo:lo + chunk, :], preferred_element_type=F32)
    y = res_ref[0] + mod[5:6] * acc_scr[...]
    if emit_h:
        modn = modn_ref[0, 0]
        _store_grouped(hn_ref, _norm_mod(y, gn_ref[0], modn[0:1], modn[1:2]).astype(BF16))
    if final:
        y = _rms(y) * fg_ref[...]
    o_ref[0] = y


def _ffn(x, mod, layer, b_off, w, *, final=False, proj=None, emit_h=False):
    n_b, seq, _ = x.shape
    tm = min(FFN_TOKEN_TILE, seq)
    wgu, wd = w["ffn_wgu"], w["ffn_wd"]
    hidden = wd.shape[1]
    chunk = FFN_CHUNK
    n_chunks = hidden // chunk
    tile = pl.BlockSpec((1, tm, D_MODEL), lambda b, i: (b, i, 0))
    args = [x, mod, w["norm2_g"], wgu, wd, w["final_g"]]
    in_specs = [
        tile,
        _mod_spec(layer, b_off),
        pl.BlockSpec((1, 1, D_MODEL), lambda b, i: (layer, 0, 0)),
        _resident((1, D_MODEL, 2 * hidden), lambda b, i: (layer, 0, 0)),
        _resident((1, hidden, D_MODEL), lambda b, i: (layer, 0, 0)),
        pl.BlockSpec((1, D_MODEL), lambda b, i: (0, 0)),
    ]
    if proj is not None:
        f, w_o, j = proj
        args += [f, w_o]
        in_specs += [_grouped_tile(tm), _resident((1, D_MODEL, D_MODEL), lambda b, i: (j, 0, 0))]
    out_shape = [jax.ShapeDtypeStruct(x.shape, F32)]
    out_specs = [tile]
    if emit_h:
        args += [mod, w["norm1_g"]]
        in_specs += [_mod_spec(layer + 1, b_off),
                     pl.BlockSpec((1, 1, D_MODEL), lambda b, i: (layer + 1, 0, 0))]
        out_shape.append(_grouped_shape(n_b, seq))
        out_specs.append(_grouped_tile(tm))
    kern = functools.partial(_ffn_kernel, n_chunks=n_chunks, chunk=chunk, final=final,
                             proj=proj is not None, emit_h=emit_h)
    out = pl.pallas_call(
        kern,
        grid=(n_b, seq // tm),
        in_specs=in_specs,
        out_specs=out_specs,
        out_shape=out_shape,
        scratch_shapes=[pltpu.VMEM((tm, D_MODEL), BF16), pltpu.VMEM((tm, D_MODEL), F32)],
        compiler_params=_params(2),
        name="ffn",
    )(*args)
    return out if emit_h else out[0]


def _hnorm_kernel(x_ref, mod_ref, g_ref, o_ref):
    mod = mod_ref[0, 0]
    _store_grouped(o_ref, _norm_mod(x_ref[0], g_ref[0], mod[0:1], mod[1:2]).astype(BF16))


def _hnorm(x, mod, layer, b_off, norm_g):
    n_b, seq, _ = x.shape
    tm = min(FFN_TOKEN_TILE, seq)
    tile = pl.BlockSpec((1, tm, D_MODEL), lambda b, i: (b, i, 0))
    return pl.pallas_call(
        _hnorm_kernel,
        grid=(n_b, seq // tm),
        in_specs=[tile, _mod_spec(layer, b_off),
                  pl.BlockSpec((1, 1, D_MODEL), lambda b, i: (layer, 0, 0))],
        out_specs=_grouped_tile(tm),
        out_shape=_grouped_shape(n_b, seq),
        compiler_params=_params(2),
        name="mixer_input_norm",
    )(x, mod, norm_g)


def _fnet_dft_kernel(h_ref, wc_ref, wa_ref, mb_ref, o_ref, rg_scr, ab_scr, v_scr,
                     *, n2, tc, pitch, scale):
    seq = h_ref.shape[2]
    gd = FNET_GROUP_DIM
    k1l = DFT_LONG // n2
    nblk = gd // LANES
    for c in range(seq // tc):
        ab = jnp.dot(h_ref[0, 0, c * tc:(c + 1) * tc, :], wc_ref[...], preferred_element_type=F32)
        for part in range(2):
            for t in range(nblk):
                lo = part * gd + t * LANES
                if pitch == n2:
                    rg_scr[part * nblk + t, c * tc:(c + 1) * tc, :] = ab[:, lo:lo + LANES]
                else:
                    for q in range(tc // n2):
                        r0 = (c * (tc // n2) + q) * pitch
                        rg_scr[part * nblk + t, r0:r0 + n2, :] = ab[q * n2:(q + 1) * n2, lo:lo + LANES]
    for j in range(n2):
        for part in range(2):
            for t in range(nblk):
                sel = rg_scr[part * nblk + t, pl.ds(j, DFT_LONG, stride=pitch), :]
                ab_scr[part, j, :, t * LANES:(t + 1) * LANES] = sel.astype(BF16)
    for j in range(n2):
        ab = ab_scr[:, j].reshape(2 * DFT_LONG, gd)
        v = jnp.dot(wa_ref[j], ab, preferred_element_type=F32)
        v_scr[:, :, j] = v.reshape(2, n2, k1l, gd).astype(BF16)
    for kb in range(n2):
        vb = v_scr[:, kb].reshape(2 * DFT_LONG, gd)
        f = jnp.dot(mb_ref[...], vb, preferred_element_type=F32) * scale
        o_ref[0, 0, :, kb] = f.reshape(n2, k1l, gd).astype(BF16)


def _fnet_tables(seq):
    n2 = seq // DFT_LONG
    k1l = DFT_LONG // n2
    two_pi = 2.0 * math.pi
    gd = FNET_GROUP_DIM
    c = jnp.arange(gd, dtype=jnp.int32)
    ang = ((c[:, None] * c[None, :]) % gd).astype(F32) * (two_pi / gd)
    inv = 1.0 / math.sqrt(gd)
    wc = jnp.concatenate([jnp.cos(ang) * inv, jnp.sin(ang) * inv], axis=1).astype(BF16)

    k1 = jnp.arange(DFT_LONG, dtype=jnp.int32)
    n1 = jnp.arange(DFT_LONG, dtype=jnp.int32)
    j2 = jnp.arange(n2, dtype=jnp.int32)
    sp = n2 * n1[None, None, :] + j2[:, None, None]
    ang = ((k1[None, :, None] * sp) % seq).astype(F32) * (two_pi / seq)
    ca, sa = jnp.cos(ang), jnp.sin(ang)
    wa = jnp.concatenate([jnp.concatenate([ca, -sa], axis=2),
                          jnp.concatenate([sa, ca], axis=2)], axis=1).astype(BF16)

    ang = ((j2[:, None] * j2[None, :]) % n2).astype(F32) * (two_pi / n2)
    eye = jnp.eye(k1l, dtype=F32)
    cb = jnp.einsum("kn,ab->kanb", jnp.cos(ang), eye).reshape(DFT_LONG, DFT_LONG)
    sb = jnp.einsum("kn,ab->kanb", jnp.sin(ang), eye).reshape(DFT_LONG, DFT_LONG)
    mb = jnp.concatenate([cb, -sb], axis=1).astype(BF16)
    return wc, wa, mb


def _fnet_dft(h):
    n_b, _, seq, _ = h.shape
    n2 = seq // DFT_LONG
    k1l = DFT_LONG // n2
    tc = min(DFT_CHANNEL_TILE, seq)
    gd = FNET_GROUP_DIM
    wc, wa, mb = _fnet_tables(seq)
    pitch = n2 + SUBLANES // 2 if n2 % SUBLANES == 0 else n2
    out = pl.pallas_call(
        functools.partial(_fnet_dft_kernel, n2=n2, tc=tc, pitch=pitch, scale=1.0 / math.sqrt(seq)),
        grid=(n_b, FNET_GROUPS),
        in_specs=[
            pl.BlockSpec((1, 1, seq, gd), lambda b, g: (b, g, 0, 0)),
            pl.BlockSpec((gd, 2 * gd), lambda b, g: (0, 0)),
            _resident((n2, 2 * DFT_LONG, 2 * DFT_LONG), lambda b, g: (0, 0, 0)),
            pl.BlockSpec((DFT_LONG, 2 * DFT_LONG), lambda b, g: (0, 0)),
        ],
        out_specs=pl.BlockSpec((1, 1, n2, n2, k1l, gd), lambda b, g: (b, g, 0, 0, 0, 0)),
        out_shape=jax.ShapeDtypeStruct((n_b, FNET_GROUPS, n2, n2, k1l, gd), BF16),
        scratch_shapes=[
            pltpu.VMEM((2 * gd // LANES, DFT_LONG * pitch, LANES), F32),
            pltpu.VMEM((2, n2, DFT_LONG, gd), BF16),
            pltpu.VMEM((2, n2, n2, k1l, gd), BF16),
        ],
        compiler_params=_params(2),
        name="fnet_dft",
    )(h, wc, wa, mb)
    return out.reshape(n_b, FNET_GROUPS, seq, gd)


def _attn_qkv_kernel(x_ref, mod_ref, g_ref, w_ref, qg_ref, kg_ref, cos_ref, sin_ref,
                     q_ref, k_ref, v_ref):
    mod = mod_ref[0, 0]
    hd = HEAD_DIM
    k0 = N_HEADS * hd
    v0 = k0 + N_KV_HEADS * hd
    sub = SUB_TILE
    for st in range(x_ref.shape[1] // sub):
        r = slice(st * sub, (st + 1) * sub)
        h = _norm_mod(x_ref[0, r, :], g_ref[0], mod[0:1], mod[1:2]).astype(BF16)
        qkv = jnp.dot(h, w_ref[0], preferred_element_type=F32)
        cos = cos_ref[r, :]
        sin = sin_ref[r, :]

        def head(t, gain, scale):
            t = _rms(t) * gain
            return ((t * cos + pltpu.roll(t, hd // 2, 1) * sin) * scale).astype(BF16)

        for i in range(N_HEADS):
            q_ref[0, r, i * hd:(i + 1) * hd] = head(qkv[:, i * hd:(i + 1) * hd], qg_ref[0],
                                                     HEAD_DIM ** -0.5 * LOG2_E)
        ones = jnp.ones((sub, hd), BF16)
        for i in range(N_KV_HEADS):
            k_ref[0, r, i * hd:(i + 1) * hd] = head(qkv[:, k0 + i * hd:k0 + (i + 1) * hd],
                                                     kg_ref[0], 1.0)
            v_ref[0, r, 2 * i * hd:(2 * i + 1) * hd] = (
                qkv[:, v0 + i * hd:v0 + (i + 1) * hd].astype(BF16))
            v_ref[0, r, (2 * i + 1) * hd:(2 * i + 2) * hd] = ones


def _attn_kernel(q_ref, k_ref, v_ref, x_ref, mod_ref, wo_ref, o_ref, o_scr):
    hd = HEAD_DIM
    for g in range(N_KV_HEADS):
        k = k_ref[0, :, g * hd:(g + 1) * hd]
        v1 = v_ref[0, :, 2 * g * hd:(2 * g + 2) * hd]
        for i in range(Q_PER_KV):
            hh = g * Q_PER_KV + i
            q = q_ref[0, :, hh * hd:(hh + 1) * hd]
            s = lax.dot_general(q, k, (((1,), (1,)), ((), ())), preferred_element_type=F32)
            p = jnp.exp2(s - jnp.max(s, axis=-1, keepdims=True)).astype(BF16)
            pv = jnp.dot(p, v1, preferred_element_type=F32)
            o = pv[:, :hd] / pv[:, hd:hd + 1]
            o_scr[:, hh * hd:(hh + 1) * hd] = o.astype(BF16)
    m = jnp.dot(o_scr[...], wo_ref[0], preferred_element_type=F32)
    o_ref[0] = x_ref[0] + mod_ref[0, 0][2:3] * m


def _rope_lane_order():
    q = HEAD_DIM // 4
    return [*range(0, q), *range(2 * q, 3 * q), *range(q, 2 * q), *range(3 * q, 4 * q)]


def _rope_tables(seq):
    pos = jnp.arange(seq, dtype=jnp.int32)
    row = (pos // GRID_W).astype(F32)
    col = (pos % GRID_W).astype(F32)
    half = HEAD_DIM // 2
    freqs = 1.0 / (ROPE_THETA ** (jnp.arange(0, half, 2, dtype=F32) / half))
    ang_r = row[:, None] * freqs[None, :]
    ang_c = col[:, None] * freqs[None, :]
    cos = jnp.concatenate([jnp.cos(ang_r), jnp.cos(ang_c), jnp.cos(ang_r), jnp.cos(ang_c)], axis=1)
    sin = jnp.concatenate([-jnp.sin(ang_r), -jnp.sin(ang_c), jnp.sin(ang_r), jnp.sin(ang_c)], axis=1)
    return cos, sin


def _attention(x, mod, layer, b_off, norm_g, w_qkv, q_g, k_g, w_o, j):
    n_b, seq, _ = x.shape
    tm = min(TOKEN_TILE, seq)
    cos, sin = _rope_tables(seq)
    kv_dim = N_KV_HEADS * HEAD_DIM
    q, k, v = pl.pallas_call(
        _attn_qkv_kernel,
        grid=(n_b, seq // tm),
        in_specs=[
            pl.BlockSpec((1, tm, D_MODEL), lambda b, i: (b, i, 0)),
            _mod_spec(layer, b_off),
            pl.BlockSpec((1, 1, D_MODEL), lambda b, i: (layer, 0, 0)),
            pl.BlockSpec((1, D_MODEL, QKV_DIM), lambda b, i: (j, 0, 0)),
            pl.BlockSpec((1, 1, HEAD_DIM), lambda b, i: (j, 0, 0)),
            pl.BlockSpec((1, 1, HEAD_DIM), lambda b, i: (j, 0, 0)),
            pl.BlockSpec((tm, HEAD_DIM), lambda b, i: (i, 0)),
            pl.BlockSpec((tm, HEAD_DIM), lambda b, i: (i, 0)),
        ],
        out_specs=[
            pl.BlockSpec((1, tm, D_MODEL), lambda b, i: (b, i, 0)),
            pl.BlockSpec((1, tm, kv_dim), lambda b, i: (b, i, 0)),
            pl.BlockSpec((1, tm, 2 * kv_dim), lambda b, i: (b, i, 0)),
        ],
        out_shape=[
            jax.ShapeDtypeStruct((n_b, seq, D_MODEL), BF16),
            jax.ShapeDtypeStruct((n_b, seq, kv_dim), BF16),
            jax.ShapeDtypeStruct((n_b, seq, 2 * kv_dim), BF16),
        ],
        compiler_params=_params(2),
        name="attn_qkv_rope",
    )(x, mod, norm_g, w_qkv, q_g, k_g, cos, sin)

    tq = min(ATTN_Q_TILE, seq)
    return pl.pallas_call(
        _attn_kernel,
        grid=(n_b, seq // tq),
        in_specs=[
            pl.BlockSpec((1, tq, D_MODEL), lambda b, i: (b, i, 0)),
            pl.BlockSpec((1, seq, kv_dim), lambda b, i: (b, 0, 0)),
            pl.BlockSpec((1, seq, 2 * kv_dim), lambda b, i: (b, 0, 0)),
            pl.BlockSpec((1, tq, D_MODEL), lambda b, i: (b, i, 0)),
            _mod_spec(layer, b_off),
            pl.BlockSpec((1, D_MODEL, D_MODEL), lambda b, i: (j, 0, 0)),
        ],
        out_specs=pl.BlockSpec((1, tq, D_MODEL), lambda b, i: (b, i, 0)),
        out_shape=jax.ShapeDtypeStruct(x.shape, F32),
        scratch_shapes=[pltpu.VMEM((tq, D_MODEL), BF16)],
        compiler_params=_params(2),
        name="attn_softmax_out",
    )(q, k, v, x, mod, w_o)


def _sgu_kernel(x_ref, mod_ref, g_ref, win_ref, lng_ref, lnb_ref, ws_ref, bias_ref, wo_ref,
                o_ref, y_scr):
    mod = mod_ref[0, 0]
    ck = SGU_CHUNK
    gd = D_MODEL // SGU_GROUPS
    bias = bias_ref[0]
    sub = SGU_SUB_TILE
    n_sub = x_ref.shape[1] // sub

    def gate_inputs(st):
        x = x_ref[0, st * sub:(st + 1) * sub, :]
        h = _norm_mod(x, g_ref[0], mod[0:1], mod[1:2]).astype(BF16)
        uv = jnp.dot(h, win_ref[0], preferred_element_type=F32)
        uv = 0.5 * uv * (1.0 + lax.erf(uv * math.sqrt(0.5)))
        u = uv[:, :D_MODEL]
        v = uv[:, D_MODEL:]
        vc = v - jnp.mean(v, axis=-1, keepdims=True)
        vn = vc * lax.rsqrt(jnp.mean(vc * vc, axis=-1, keepdims=True) + EPS)
        return u, (vn * lng_ref[0] + lnb_ref[0]).astype(BF16)

    def gate_and_project(st, u, vn):
        r0 = st * sub
        n_ck = sub // ck
        for g in range(SGU_GROUPS):
            cols = slice(g * gd, (g + 1) * gd)
            rhs = jnp.concatenate([vn[c * ck:(c + 1) * ck, cols] for c in range(n_ck)], axis=1)
            sv = jnp.dot(ws_ref[0, g], rhs, preferred_element_type=F32)
            for c in range(n_ck):
                y_scr[r0 + c * ck:r0 + (c + 1) * ck, cols] = (
                    u[c * ck:(c + 1) * ck, cols]
                    * (sv[:, c * gd:(c + 1) * gd] + bias[:, cols])).astype(BF16)
        m = jnp.dot(y_scr[r0:r0 + sub, :], wo_ref[0], preferred_element_type=F32)
        o_ref[0, r0:r0 + sub, :] = x_ref[0, r0:r0 + sub, :] + mod[2:3] * m

    pending = gate_inputs(0)
    for st in range(1, n_sub):
        nxt = gate_inputs(st)
        gate_and_project(st - 1, *pending)
        pending = nxt
    gate_and_project(n_sub - 1, *pending)


def _sgu(x, mod, layer, b_off, norm_g, w_in, ln_g, ln_b, w_s, bias, w_o, j):
    n_b, seq, _ = x.shape
    tm = min(2 * SGU_SUB_TILE, seq)
    return pl.pallas_call(
        _sgu_kernel,
        grid=(n_b, seq // tm),
        in_specs=[
            pl.BlockSpec((1, tm, D_MODEL), lambda b, i: (b, i, 0)),
            _mod_spec(layer, b_off),
            pl.BlockSpec((1, 1, D_MODEL), lambda b, i: (layer, 0, 0)),
            pl.BlockSpec((1, D_MODEL, 2 * D_MODEL), lambda b, i: (j, 0, 0)),
            pl.BlockSpec((1, 1, D_MODEL), lambda b, i: (j, 0, 0)),
            pl.BlockSpec((1, 1, D_MODEL), lambda b, i: (j, 0, 0)),
            pl.BlockSpec((1, SGU_GROUPS, SGU_CHUNK, SGU_CHUNK), lambda b, i: (j, 0, 0, 0)),
            pl.BlockSpec((1, SGU_CHUNK, D_MODEL), lambda b, i: (j, 0, 0)),
            pl.BlockSpec((1, D_MODEL, D_MODEL), lambda b, i: (j, 0, 0)),
        ],
        out_specs=pl.BlockSpec((1, tm, D_MODEL), lambda b, i: (b, i, 0)),
        out_shape=jax.ShapeDtypeStruct(x.shape, F32),
        scratch_shapes=[pltpu.VMEM((tm, D_MODEL), BF16)],
        compiler_params=_params(2),
        name="sgu",
    )(x, mod, norm_g, w_in, ln_g, ln_b, w_s, bias, w_o)


def _trunk(x, mod, b_off, w):
    h_next = None
    for i in range(DEPTH):
        kind, j = i % 3, i // 3
        last = i == DEPTH - 1
        emit_h = (not last) and (i + 1) % 3 == 0
        proj = None
        if kind == 0:
            h = h_next if h_next is not None else _hnorm(x, mod, i, b_off, w["norm1_g"])
            proj = (_fnet_dft(h), w["fnet_w_o"], j)
        elif kind == 1:
            x = _attention(x, mod, i, b_off, w["norm1_g"], w["attn_w_qkv"], w["attn_q_g"],
                           w["attn_k_g"], w["attn_w_o"], j)
        else:
            x = _sgu(x, mod, i, b_off, w["norm1_g"], w["sgu_w_in"], w["sgu_ln_g"], w["sgu_ln_b"],
                     w["sgu_w_s"], w["sgu_bias"], w["sgu_w_o"], j)
        out = _ffn(x, mod, i, b_off, w, final=last, proj=proj, emit_h=emit_h)
        x, h_next = out if emit_h else (out, None)
    return x


def kernel(x_prompt, x_sample, c_prompt, c_sample, norm1_g, norm2_g, w_ada, b_ada, fnet_w_o,
           attn_w_qkv, attn_q_g, attn_k_g, attn_w_o, sgu_w_in, sgu_ln_g, sgu_ln_b, sgu_w_s, sgu_b_s,
           sgu_w_o, ffn_w_gu, ffn_w_down, final_g):
    depth = norm1_g.shape[0]
    gd = D_MODEL // SGU_GROUPS
    lane_order = np.asarray(_rope_lane_order())
    n_rot = (N_HEADS + N_KV_HEADS) * HEAD_DIM
    qkv_cols = np.concatenate([(np.arange(n_rot) // HEAD_DIM) * HEAD_DIM + np.tile(lane_order, n_rot // HEAD_DIM),
                               np.arange(n_rot, QKV_DIM)])
    sgu_bias = jnp.repeat(jnp.swapaxes(sgu_b_s, 1, 2), gd, axis=2)
    w = {
        "norm1_g": norm1_g.reshape(depth, 1, D_MODEL),
        "norm2_g": norm2_g.reshape(depth, 1, D_MODEL),
        "fnet_w_o": fnet_w_o.astype(BF16),
        "attn_w_qkv": attn_w_qkv[:, :, qkv_cols].astype(BF16),
        "attn_q_g": attn_q_g[:, lane_order].reshape(-1, 1, HEAD_DIM),
        "attn_k_g": attn_k_g[:, lane_order].reshape(-1, 1, HEAD_DIM),
        "attn_w_o": attn_w_o.astype(BF16),
        "sgu_w_in": sgu_w_in.astype(BF16),
        "sgu_ln_g": sgu_ln_g.reshape(-1, 1, D_MODEL),
        "sgu_ln_b": sgu_ln_b.reshape(-1, 1, D_MODEL),
        "sgu_w_s": sgu_w_s.astype(BF16),
        "sgu_bias": sgu_bias,
        "sgu_w_o": sgu_w_o.astype(BF16),
        "ffn_wgu": ffn_w_gu.astype(BF16),
        "ffn_wd": ffn_w_down.astype(BF16),
        "final_g": final_g.reshape(1, D_MODEL),
    }
    c_all = jnp.concatenate([c_prompt, c_sample], axis=0)
    mod = _modulation(c_all, w_ada, b_ada)
    y_prompt = _trunk(x_prompt, mod, 0, w)
    y_sample = _trunk(x_sample, mod, x_prompt.shape[0], w)
    return (y_prompt, y_sample)
```
